```python
import math
import jax, jax.numpy as jnp
from jax import lax
import numpy as np

D_MODEL = 1024
BATCH = 2
SEQ = 8192
DEPTH = 1

N_HEADS = 8
HEAD_DIM = 128
ATTN_WIDTH = N_HEADS * HEAD_DIM
MOBA_BLOCK = 256
MOBA_TOPK = 3
Q_CHUNK = 64
ROPE_THETA = 500000.0
ROPE_DIM = HEAD_DIM // 4
CONV_WIDTH = D_MODEL
CONV_K = 3
D_FF = ((8 * D_MODEL // 3 + 255) // 256) * 256
EPS = 1e-6
NEG = -1e30

SPLIT_WIDTHS = [ATTN_WIDTH, ATTN_WIDTH, ATTN_WIDTH,
                CONV_WIDTH, CONV_WIDTH, CONV_WIDTH,
                D_MODEL, D_MODEL]
IN_WIDTH = int(sum(SPLIT_WIDTHS))
SPLIT_IDX = [int(i) for i in np.cumsum(SPLIT_WIDTHS)[:-1]]

kernel_name = "hybrid_moba_shortconv_gated_block"


def rmsnorm(x, g):
    xf = x.astype(jnp.float32)
    y = xf * lax.rsqrt(jnp.mean(xf * xf, axis=-1, keepdims=True) + EPS)
    return (y * g.astype(jnp.float32)).astype(x.dtype)


def partial_rope(t, positions):
    rot, rest = t[..., :ROPE_DIM], t[..., ROPE_DIM:]
    inv_freq = ROPE_THETA ** (-jnp.arange(0, ROPE_DIM, 2, dtype=jnp.float32) / ROPE_DIM)
    ang = positions.astype(jnp.float32)[:, None] * inv_freq[None, :]
    cos, sin = jnp.cos(ang), jnp.sin(ang)
    r = rot.astype(jnp.float32)
    r1, r2 = r[..., : ROPE_DIM // 2], r[..., ROPE_DIM // 2:]
    out = jnp.concatenate([r1 * cos - r2 * sin, r2 * cos + r1 * sin], axis=-1)
    return jnp.concatenate([out.astype(t.dtype), rest], axis=-1)


def moba_attention(q, k, v):
    B, H, S, Dh = q.shape
    S_pad = -(-S // MOBA_BLOCK) * MOBA_BLOCK
    pad = [(0, 0), (0, 0), (0, S_pad - S), (0, 0)]
    q_p, k_p, v_p = jnp.pad(q, pad), jnp.pad(k, pad), jnp.pad(v, pad)
    NB = S_pad // MOBA_BLOCK
    NC = S_pad // Q_CHUNK
    topk = min(MOBA_TOPK, NB)
    scale = 1.0 / math.sqrt(Dh)
    kb = k_p.reshape(B, H, NB, MOBA_BLOCK, Dh)
    vb = v_p.reshape(B, H, NB, MOBA_BLOCK, Dh)
    kmean = kb.astype(jnp.float32).mean(axis=3)
    bi = jnp.arange(B)[:, None, None, None]
    hi = jnp.arange(H)[None, :, None, None]
    q_chunks = q_p.reshape(B, H, NC, Q_CHUNK, Dh).transpose(2, 0, 1, 3, 4)

    def step(args):
        c, qc = args
        q0 = c * Q_CHUNK
        blk = q0 // MOBA_BLOCK
        gate = jnp.einsum('bhqd,bhnd->bhqn', qc.astype(jnp.float32), kmean)
        gate = jnp.where(jnp.arange(NB) < blk, gate, NEG)
        _, idx = lax.top_k(gate, topk)
        valid = jnp.arange(topk) < blk
        kg = kb[bi, hi, idx]
        vg = vb[bi, hi, idx]
        s_past = jnp.einsum('bhqd,bhqjkd->bhqjk', qc, kg).astype(jnp.float32) * scale
        s_past = jnp.where(valid[:, None], s_past, NEG)
        k_own = lax.dynamic_slice_in_dim(k_p, blk * MOBA_BLOCK, MOBA_BLOCK, axis=2)
        v_own = lax.dynamic_slice_in_dim(v_p, blk * MOBA_BLOCK, MOBA_BLOCK, axis=2)
        s_own = jnp.einsum('bhqd,bhkd->bhqk', qc, k_own).astype(jnp.float32) * scale
        causal = (q0 + jnp.arange(Q_CHUNK))[:, None] >= (blk * MOBA_BLOCK + jnp.arange(MOBA_BLOCK))[None, :]
        s_own = jnp.where(causal, s_own, NEG)
        logits = jnp.concatenate([s_past.reshape(B, H, Q_CHUNK, topk * MOBA_BLOCK), s_own], axis=-1)
        p = jax.nn.softmax(logits, axis=-1).astype(v.dtype)
        p_past = p[..., : topk * MOBA_BLOCK].reshape(B, H, Q_CHUNK, topk, MOBA_BLOCK)
        p_own = p[..., topk * MOBA_BLOCK:]
        return (jnp.einsum('bhqjk,bhqjkd->bhqd', p_past, vg)
                + jnp.einsum('bhqk,bhkd->bhqd', p_own, v_own))

    out = lax.map(step, (jnp.arange(NC, dtype=jnp.int32), q_chunks))
    out = out.transpose(1, 2, 0, 3, 4).reshape(B, H, S_pad, Dh)
    return out[:, :, :S]


def causal_depthwise_conv(z, w):
    C = z.shape[-1]
    return lax.conv_general_dilated(
        z, w[:, None, :].astype(z.dtype), window_strides=(1,),
        padding=[(CONV_K - 1, 0)], dimension_numbers=('NWC', 'WIO', 'NWC'),
        feature_group_count=C)


def setup_inputs(seed: int = 0) -> dict:
    key = jax.random.key(seed)
    ks = jax.random.split(key, 12)
    f32 = jnp.float32
    def nrm(k, shape, fan_in):
        return jax.random.normal(k, shape, f32) * (fan_in ** -0.5)
    def gain(k, n):
        return 1.0 + 0.05 * jax.random.normal(k, (DEPTH, n), f32)
    return {
        "x": jax.random.normal(ks[0], (BATCH, SEQ, D_MODEL), f32),
        "g_mix": gain(ks[1], D_MODEL),
        "w_in": nrm(ks[2], (DEPTH, D_MODEL, IN_WIDTH), D_MODEL),
        "conv_w": nrm(ks[3], (DEPTH, CONV_K, CONV_WIDTH), CONV_K),
        "w_attn_branch": nrm(ks[4], (DEPTH, ATTN_WIDTH, D_MODEL), ATTN_WIDTH),
        "w_conv_branch": nrm(ks[5], (DEPTH, CONV_WIDTH, D_MODEL), CONV_WIDTH),
        "w_out": nrm(ks[6], (DEPTH, D_MODEL, D_MODEL), D_MODEL),
        "g_ffn": gain(ks[7], D_MODEL),
        "w_gate_up": nrm(ks[8], (DEPTH, D_MODEL, 2 * D_FF), D_MODEL),
        "w_down": nrm(ks[9], (DEPTH, D_FF, D_MODEL), D_FF),
        "g_final": 1.0 + 0.05 * jax.random.normal(ks[10], (D_MODEL,), f32),
    }


def reference(x, g_mix, w_in, conv_w, w_attn_branch, w_conv_branch, w_out,
              g_ffn, w_gate_up, w_down, g_final):
    B, S, _ = x.shape
    positions = jnp.arange(S, dtype=jnp.int32)
    h = x
    for l in range(DEPTH):
        u = rmsnorm(h, g_mix[l])
        proj = u @ w_in[l]
        q, k, v, c_gate, b_gate, xc, ga, gc = jnp.split(proj, SPLIT_IDX, axis=-1)
        to_heads = lambda t: t.reshape(B, S, N_HEADS, HEAD_DIM).transpose(0, 2, 1, 3)
        qh = partial_rope(to_heads(q), positions)
        kh = partial_rope(to_heads(k), positions)
        vh = to_heads(v)
        attn = moba_attention(qh, kh, vh).transpose(0, 2, 1, 3).reshape(B, S, ATTN_WIDTH)
        y_attn = attn @ w_attn_branch[l]
        zc = causal_depthwise_conv(c_gate * xc, conv_w[l])
        y_conv = (b_gate * zc) @ w_conv_branch[l]
        merged = jax.nn.sigmoid(ga) * y_attn + jax.nn.sigmoid(gc) * y_conv
        h = h + merged @ w_out[l]
        u2 = rmsnorm(h, g_ffn[l])
        gate, up = jnp.split(u2 @ w_gate_up[l], 2, axis=-1)
        h = h + (jax.nn.silu(gate) * up) @ w_down[l]
    return rmsnorm(h, g_final)
```

```python
import functools
import math

import jax
import jax.numpy as jnp
import numpy as np
from jax import lax
from jax.experimental import pallas as pl
from jax.experimental.pallas import tpu as pltpu

D_MODEL = 1024
N_HEADS = 8
HEAD_DIM = 128
MOBA_BLOCK = 256
MOBA_TOPK = 3
ROPE_THETA = 500000.0
ROPE_DIM = HEAD_DIM // 4
ROPE_HALF = ROPE_DIM // 2
CONV_K = 3
D_FF = 2816
EPS = 1e-6
NEG = -1e30
N_SPLITS = 8
IN_WIDTH = N_SPLITS * D_MODEL

VMEM_LIMIT_BYTES = 56 * 1024 * 1024
PROJ_TM = 1024
MIX_TM = 512
FFN_TM = 512
FFN_CHUNKS = 2
HALO = 8


def _rms(xf, g):
    return xf * lax.rsqrt(jnp.mean(xf * xf, axis=-1, keepdims=True) + EPS) * g


def _in_proj_kernel(x_ref, g_ref, w_ref, cos_ref, sina_ref, sinb_ref, o_ref, u_ref):
    j = pl.program_id(1)

    @pl.when(j == 0)
    def _():
        u_ref[...] = _rms(x_ref[...], g_ref[...]).astype(jnp.bfloat16)

    acc = jnp.dot(u_ref[...], w_ref[...], preferred_element_type=jnp.float32)

    def rope_store(scale):
        cos, sina, sinb = cos_ref[...], sina_ref[...], sinb_ref[...]
        for h in range(N_HEADS):
            t = acc[:, h * HEAD_DIM:(h + 1) * HEAD_DIM]
            r = (t * cos + pltpu.roll(t, HEAD_DIM - ROPE_HALF, 1) * sina
                 + pltpu.roll(t, ROPE_HALF, 1) * sinb)
            o_ref[:, h * HEAD_DIM:(h + 1) * HEAD_DIM] = (r * scale).astype(o_ref.dtype)

    @pl.when(j == 0)
    def _():
        rope_store(1.0 / math.sqrt(HEAD_DIM))

    @pl.when(j == 1)
    def _():
        rope_store(1.0)

    @pl.when(j >= 2)
    def _():
        o_ref[...] = acc.astype(o_ref.dtype)


def _in_proj(x2, g_mix, w_in_bf16, cos, sina, sinb, seq):
    t = x2.shape[0]
    tiles_per_seq = seq // PROJ_TM
    tab_spec = pl.BlockSpec((PROJ_TM, HEAD_DIM), lambda i, j: (i % tiles_per_seq, 0))
    return pl.pallas_call(
        _in_proj_kernel,
        out_shape=jax.ShapeDtypeStruct((t, IN_WIDTH), jnp.bfloat16),
        grid=(t // PROJ_TM, N_SPLITS),
        in_specs=[
            pl.BlockSpec((PROJ_TM, D_MODEL), lambda i, j: (i, 0)),
            pl.BlockSpec((1, D_MODEL), lambda i, j: (0, 0)),
            pl.BlockSpec((D_MODEL, D_MODEL), lambda i, j: (0, j)),
            tab_spec, tab_spec, tab_spec,
        ],
        out_specs=pl.BlockSpec((PROJ_TM, D_MODEL), lambda i, j: (i, j)),
        scratch_shapes=[pltpu.VMEM((PROJ_TM, D_MODEL), jnp.bfloat16)],
        compiler_params=pltpu.CompilerParams(
            dimension_semantics=("arbitrary", "arbitrary"), vmem_limit_bytes=VMEM_LIMIT_BYTES),
        name="in_proj",
    )(x2, g_mix, w_in_bf16, cos, sina, sinb)


def _moba_kernel(q_ref, k_ref, v_ref, o_ref, vt_ref, kmean_ref, sel_ref, *, n_blocks):
    i = pl.program_id(2)
    f32 = jnp.float32

    @pl.when(i == 0)
    def _():
        for c in range(n_blocks):
            kc = k_ref[c * MOBA_BLOCK:(c + 1) * MOBA_BLOCK, :].astype(f32)
            kmean_ref[c:c + 1, :] = jnp.mean(kc, axis=0, keepdims=True)
            vc = v_ref[c * MOBA_BLOCK:(c + 1) * MOBA_BLOCK, :].astype(f32)
            vt_ref[c] = vc.T.astype(vt_ref.dtype)

    qt = q_ref[...].astype(f32).T.astype(jnp.bfloat16)

    gate = jnp.dot(kmean_ref[...].astype(jnp.bfloat16), qt, preferred_element_type=f32)
    blk = lax.broadcasted_iota(jnp.int32, gate.shape, 0)
    past = blk < i
    g = jnp.where(past, gate, NEG)
    sel = jnp.zeros(gate.shape, f32)
    for _ in range(MOBA_TOPK):
        mx = jnp.max(g, axis=0, keepdims=True)
        first = jnp.min(jnp.where(g == mx, blk, n_blocks), axis=0, keepdims=True)
        pick = blk == first
        sel = jnp.where(pick, 1.0, sel)
        g = jnp.where(pick, -jnp.inf, g)
    sel_ref[...] = jnp.where(past, sel, 0.0)

    own = pl.multiple_of(i * MOBA_BLOCK, MOBA_BLOCK)
    st = jnp.dot(k_ref[pl.ds(own, MOBA_BLOCK), :], qt, preferred_element_type=f32)
    key_idx = lax.broadcasted_iota(jnp.int32, st.shape, 0)
    qry_idx = lax.broadcasted_iota(jnp.int32, st.shape, 1)
    st = jnp.where(key_idx <= qry_idx, st, NEG)
    m0 = jnp.max(st, axis=0, keepdims=True)
    p = jnp.exp(st - m0)
    l0 = jnp.sum(p, axis=0, keepdims=True)
    acc0 = jnp.dot(vt_ref[i], p.astype(jnp.bfloat16), preferred_element_type=f32)

    def body(j, carry):
        m, l, acc = carry
        kj = k_ref[pl.ds(pl.multiple_of(j * MOBA_BLOCK, MOBA_BLOCK), MOBA_BLOCK), :]
        s = jnp.dot(kj, qt, preferred_element_type=f32)
        on = sel_ref[pl.ds(j, 1), :] > 0.0
        m_new = jnp.maximum(m, jnp.where(on, jnp.max(s, axis=0, keepdims=True), NEG))
        alpha = jnp.exp(m - m_new)
        pj = jnp.exp(s - jnp.where(on, m_new, -NEG))
        l = alpha * l + jnp.sum(pj, axis=0, keepdims=True)
        acc = alpha * acc + jnp.dot(vt_ref[j], pj.astype(jnp.bfloat16), preferred_element_type=f32)
        return m_new, l, acc

    _, l, acc = lax.fori_loop(0, i, body, (m0, l0, acc0))
    o_ref[...] = (acc / l).T.astype(o_ref.dtype)


def _moba_attention(proj3):
    b, s, _ = proj3.shape
    n_blocks = s // MOBA_BLOCK
    return pl.pallas_call(
        functools.partial(_moba_kernel, n_blocks=n_blocks),
        out_shape=jax.ShapeDtypeStruct((b, s, N_HEADS * HEAD_DIM), jnp.bfloat16),
        grid=(b, N_HEADS, n_blocks),
        in_specs=[
            pl.BlockSpec((None, MOBA_BLOCK, HEAD_DIM), lambda bi, h, i: (bi, i, h)),
            pl.BlockSpec((None, s, HEAD_DIM), lambda bi, h, i: (bi, 0, N_HEADS + h)),
            pl.BlockSpec((None, s, HEAD_DIM), lambda bi, h, i: (bi, 0, 2 * N_HEADS + h)),
        ],
        out_specs=pl.BlockSpec((None, MOBA_BLOCK, HEAD_DIM), lambda bi, h, i: (bi, i, h)),
        scratch_shapes=[
            pltpu.VMEM((n_blocks, HEAD_DIM, MOBA_BLOCK), jnp.bfloat16),
            pltpu.VMEM((n_blocks, HEAD_DIM), jnp.float32),
            pltpu.VMEM((n_blocks, MOBA_BLOCK), jnp.float32),
        ],
        compiler_params=pltpu.CompilerParams(
            dimension_semantics=("arbitrary", "arbitrary", "arbitrary"), vmem_limit_bytes=VMEM_LIMIT_BYTES),
        name="moba_attn",
    )(proj3, proj3, proj3)


def _mix_kernel(x_ref, attn_ref, cg_ref, bg_ref, xc_ref, ga_ref, gc_ref, cgh_ref, xch_ref,
                cw_ref, wab_ref, wcb_ref, wout_ref, o_ref, *, tiles_per_seq):
    f32 = jnp.float32
    i = pl.program_id(0)
    cx = cg_ref[...].astype(f32) * xc_ref[...].astype(f32)
    live = (i % tiles_per_seq != 0).astype(f32)
    halo = cgh_ref[...].astype(f32) * xch_ref[...].astype(f32) * live
    row = lax.broadcasted_iota(jnp.int32, cx.shape, 0)
    cw = cw_ref[...]
    z = cw[CONV_K - 1:CONV_K, :] * cx
    for d in range(1, CONV_K):
        shifted = pltpu.roll(cx, d, 0)
        for r in range(d):
            shifted = jnp.where(row == r, halo[HALO - d + r:HALO - d + r + 1, :], shifted)
        z = z + cw[CONV_K - 1 - d:CONV_K - d, :] * shifted
    yc = jnp.dot((bg_ref[...].astype(f32) * z).astype(jnp.bfloat16), wcb_ref[...], preferred_element_type=f32)
    ya = jnp.dot(attn_ref[...], wab_ref[...], preferred_element_type=f32)
    merged = jax.nn.sigmoid(ga_ref[...].astype(f32)) * ya + jax.nn.sigmoid(gc_ref[...].astype(f32)) * yc
    o_ref[...] = x_ref[...] + jnp.dot(merged.astype(jnp.bfloat16), wout_ref[...], preferred_element_type=f32)


def _mix(x2, attn2, proj, conv_w, wab, wcb, wout, seq):
    t = x2.shape[0]
    tiles_per_seq = seq // MIX_TM
    halo_per_tile = MIX_TM // HALO

    def col(c):
        return pl.BlockSpec((MIX_TM, D_MODEL), lambda i: (i, c))

    def halo(c):
        return pl.BlockSpec((HALO, D_MODEL), lambda i: (jnp.maximum(i * halo_per_tile - 1, 0), c))

    wspec = pl.BlockSpec((D_MODEL, D_MODEL), lambda i: (0, 0))
    return pl.pallas_call(
        functools.partial(_mix_kernel, tiles_per_seq=tiles_per_seq),
        out_shape=jax.ShapeDtypeStruct((t, D_MODEL), jnp.float32),
        grid=(t // MIX_TM,),
        in_specs=[
            pl.BlockSpec((MIX_TM, D_MODEL), lambda i: (i, 0)),
            pl.BlockSpec((MIX_TM, D_MODEL), lambda i: (i, 0)),
            col(3), col(4), col(5), col(6), col(7),
            halo(3), halo(5),
            pl.BlockSpec((CONV_K, D_MODEL), lambda i: (0, 0)),
            wspec, wspec, wspec,
        ],
        out_specs=pl.BlockSpec((MIX_TM, D_MODEL), lambda i: (i, 0)),
        compiler_params=pltpu.CompilerParams(
            dimension_semantics=("arbitrary",), vmem_limit_bytes=VMEM_LIMIT_BYTES),
        name="mix",
    )(x2, attn2, proj, proj, proj, proj, proj, proj, proj, conv_w, wab, wcb, wout)


def _ffn_kernel(h_ref, g_ref, wgu_ref, wd_ref, gf_ref, o_ref):
    f32 = jnp.float32
    h = h_ref[...]
    u = _rms(h, g_ref[...]).astype(jnp.bfloat16)
    ck = D_FF // FFN_CHUNKS
    acc = h
    for c in range(FFN_CHUNKS):
        gate = jnp.dot(u, wgu_ref[:, c * ck:(c + 1) * ck], preferred_element_type=f32)
        up = jnp.dot(u, wgu_ref[:, D_FF + c * ck:D_FF + (c + 1) * ck], preferred_element_type=f32)
        act = (gate * jax.nn.sigmoid(gate) * up).astype(jnp.bfloat16)
        acc = acc + jnp.dot(act, wd_ref[c * ck:(c + 1) * ck, :], preferred_element_type=f32)
    o_ref[...] = _rms(acc, gf_ref[...])


def _ffn(h1, g_ffn, wgu, wd, g_final):
    t = h1.shape[0]
    return pl.pallas_call(
        _ffn_kernel,
        out_shape=jax.ShapeDtypeStruct((t, D_MODEL), jnp.float32),
        grid=(t // FFN_TM,),
        in_specs=[
            pl.BlockSpec((FFN_TM, D_MODEL), lambda i: (i, 0)),
            pl.BlockSpec((1, D_MODEL), lambda i: (0, 0)),
            pl.BlockSpec((D_MODEL, 2 * D_FF), lambda i: (0, 0), pipeline_mode=pl.Buffered(1)),
            pl.BlockSpec((D_FF, D_MODEL), lambda i: (0, 0), pipeline_mode=pl.Buffered(1)),
            pl.BlockSpec((1, D_MODEL), lambda i: (0, 0)),
        ],
        out_specs=pl.BlockSpec((FFN_TM, D_MODEL), lambda i: (i, 0)),
        compiler_params=pltpu.CompilerParams(
            dimension_semantics=("arbitrary",), vmem_limit_bytes=VMEM_LIMIT_BYTES),
        name="ffn",
    )(h1, g_ffn, wgu, wd, g_final)


def _rope_tables(seq):
    inv_freq = ROPE_THETA ** (-jnp.arange(0, ROPE_DIM, 2, dtype=jnp.float32) / ROPE_DIM)
    ang = jnp.arange(seq, dtype=jnp.int32).astype(jnp.float32)[:, None] * inv_freq[None, :]
    cos, sin = jnp.cos(ang), jnp.sin(ang)
    rest = HEAD_DIM - ROPE_DIM
    cos_t = jnp.concatenate([cos, cos, jnp.ones((seq, rest), jnp.float32)], axis=1)
    zeros_h = jnp.zeros((seq, ROPE_HALF), jnp.float32)
    zeros_r = jnp.zeros((seq, rest), jnp.float32)
    sina = jnp.concatenate([-sin, zeros_h, zeros_r], axis=1)
    sinb = jnp.concatenate([zeros_h, sin, zeros_r], axis=1)
    return cos_t, sina, sinb


def kernel(x, g_mix, w_in, conv_w, w_attn_branch, w_conv_branch, w_out, g_ffn, w_gate_up, w_down, g_final):
    b, s, d = x.shape
    depth = w_in.shape[0]
    assert d == D_MODEL and s % PROJ_TM == 0 and s % MOBA_BLOCK == 0
    bf16 = jnp.bfloat16
    cos, sina, sinb = _rope_tables(s)
    h = x.reshape(b * s, d)
    for l in range(depth):
        proj = _in_proj(h, g_mix[l][None, :], w_in[l].astype(bf16), cos, sina, sinb, s)
        attn = _moba_attention(proj.reshape(b, s, IN_WIDTH))
        h = _mix(h, attn.reshape(b * s, d), proj, conv_w[l], w_attn_branch[l].astype(bf16),
                 w_conv_branch[l].astype(bf16), w_out[l].astype(bf16), s)
        last = l == depth - 1
        assert last, "only DEPTH == 1 is supported"
        h = _ffn(h, g_ffn[l][None, :], w_gate_up[l].astype(bf16), w_down[l].astype(bf16), g_final[None, :])
    return h.reshape(b, s, d)
```

```python
import functools
import math

import jax
import jax.numpy as jnp
import numpy as np
from jax import lax
from jax.experimental import pallas as pl
from jax.experimental.pallas import tpu as pltpu

D_MODEL = 1024
N_HEADS = 8
HEAD_DIM = 128
MOBA_BLOCK = 256
MOBA_TOPK = 3
ROPE_THETA = 500000.0
ROPE_DIM = HEAD_DIM // 4
ROPE_HALF = ROPE_DIM // 2
CONV_K = 3
D_FF = 2816
EPS = 1e-6
NEG = -1e30
N_SPLITS = 8
IN_WIDTH = N_SPLITS * D_MODEL

VMEM_LIMIT_BYTES = 56 * 1024 * 1024
PROJ_TM = 1024
MIX_TM = 512
FFN_TM = 512
FFN_CHUNKS = 2
ATTN_HEADS = 4
SUBLANES = 8
HALO = SUBLANES


def _rms(xf, g):
    return xf * lax.rsqrt(jnp.mean(xf * xf, axis=-1, keepdims=True) + EPS) * g


def _in_proj_kernel(x_ref, g_ref, w_ref, cos_ref, sina_ref, sinb_ref, o_ref, u_ref):
    j = pl.program_id(1)

    @pl.when(j == 0)
    def _():
        u_ref[...] = _rms(x_ref[...], g_ref[...]).astype(jnp.bfloat16)

    acc = jnp.dot(u_ref[...], w_ref[...], preferred_element_type=jnp.float32)

    def rope_store(scale):
        cos, sina, sinb = cos_ref[...], sina_ref[...], sinb_ref[...]
        for h in range(N_HEADS):
            t = acc[:, h * HEAD_DIM:(h + 1) * HEAD_DIM]
            r = (t * cos + pltpu.roll(t, HEAD_DIM - ROPE_HALF, 1) * sina
                 + pltpu.roll(t, ROPE_HALF, 1) * sinb)
            o_ref[:, h * HEAD_DIM:(h + 1) * HEAD_DIM] = (r * scale).astype(o_ref.dtype)

    @pl.when(j == 0)
    def _():
        rope_store(math.log2(math.e) / math.sqrt(HEAD_DIM))

    @pl.when(j == 1)
    def _():
        rope_store(1.0)

    @pl.when(j >= 2)
    def _():
        o_ref[...] = acc.astype(o_ref.dtype)


def _in_proj(x2, g_mix, w_in_bf16, cos, sina, sinb, seq):
    t = x2.shape[0]
    tiles_per_seq = seq // PROJ_TM
    tab_spec = pl.BlockSpec((PROJ_TM, HEAD_DIM), lambda i, j: (i % tiles_per_seq, 0))
    return pl.pallas_call(
        _in_proj_kernel,
        out_shape=jax.ShapeDtypeStruct((t, IN_WIDTH), jnp.bfloat16),
        grid=(t // PROJ_TM, N_SPLITS),
        in_specs=[
            pl.BlockSpec((PROJ_TM, D_MODEL), lambda i, j: (i, 0)),
            pl.BlockSpec((1, D_MODEL), lambda i, j: (0, 0)),
            pl.BlockSpec((D_MODEL, D_MODEL), lambda i, j: (0, j)),
            tab_spec, tab_spec, tab_spec,
        ],
        out_specs=pl.BlockSpec((PROJ_TM, D_MODEL), lambda i, j: (i, j)),
        scratch_shapes=[pltpu.VMEM((PROJ_TM, D_MODEL), jnp.bfloat16)],
        compiler_params=pltpu.CompilerParams(
            dimension_semantics=("arbitrary", "arbitrary"), vmem_limit_bytes=VMEM_LIMIT_BYTES),
        name="in_proj",
    )(x2, g_mix, w_in_bf16, cos, sina, sinb)


def _reduce_rows(x, op, reduce_fn):
    rows = x.shape[0]
    while rows > SUBLANES and rows % (2 * SUBLANES) == 0:
        rows //= 2
        x = op(x[:rows], x[rows:])
    return reduce_fn(x, axis=0, keepdims=True)


def _moba_kernel(q_ref, k_ref, v_ref, o_ref, vt_ref, kmean_ref, sel_ref, qt_ref, *slot_refs, n_blocks):
    i = pl.program_id(2)
    f32, bf16 = jnp.float32, jnp.bfloat16
    heads = range(ATTN_HEADS)
    hsl = [slice(h * HEAD_DIM, (h + 1) * HEAD_DIM) for h in heads]
    nh = ATTN_HEADS
    s_ref = [slot_refs[0:nh], slot_refs[nh:2 * nh]]
    p_ref = [slot_refs[2 * nh:3 * nh], slot_refs[3 * nh:4 * nh]]
    acc_ref = slot_refs[4 * nh:5 * nh]

    @pl.when(i == 0)
    def _():
        for h in heads:
            for c in range(n_blocks):
                rows = slice(c * MOBA_BLOCK, (c + 1) * MOBA_BLOCK)
                kmean_ref[h, c:c + 1, :] = jnp.mean(k_ref[rows, hsl[h]].astype(f32), axis=0, keepdims=True)
                vt_ref[h, :, rows] = v_ref[rows, hsl[h]].astype(f32).T.astype(bf16)

    def krows(blk):
        return pl.ds(pl.multiple_of(blk * MOBA_BLOCK, MOBA_BLOCK), MOBA_BLOCK)

    def qk(blk, slot):
        for h in heads:
            s_ref[slot][h][...] = jnp.dot(k_ref[krows(blk), hsl[h]], qt_ref[h], preferred_element_type=f32)

    def pv(blk, slot):
        return [jnp.dot(vt_ref[h, :, krows(blk)], p_ref[slot][h][...], preferred_element_type=f32) for h in heads]

    def accumulate(pvs, alphas):
        for h in heads:
            acc_ref[h][...] = alphas[h] * acc_ref[h][...] + pvs[h]

    for h in heads:
        qt_ref[h] = q_ref[:, hsl[h]].astype(f32).T.astype(bf16)
    gates = [jnp.dot(kmean_ref[h].astype(bf16), qt_ref[h], preferred_element_type=f32) for h in heads]
    own_s = [jnp.dot(k_ref[krows(i), hsl[h]], qt_ref[h], preferred_element_type=f32) for h in heads]
    qk(0, 0)

    state = []
    for h in heads:
        blk = lax.broadcasted_iota(jnp.int32, gates[h].shape, 0)
        past = blk < i
        g = jnp.where(past, gates[h], NEG)
        sel = jnp.zeros(g.shape, f32)
        for _ in range(MOBA_TOPK):
            mx = jnp.max(g, axis=0, keepdims=True)
            first = jnp.min(jnp.where(g == mx, blk, n_blocks), axis=0, keepdims=True)
            pick = blk == first
            sel = jnp.where(pick, 1.0, sel)
            g = jnp.where(pick, -jnp.inf, g)
        sel_ref[h] = jnp.where(past, sel, 0.0)

        key_idx = lax.broadcasted_iota(jnp.int32, own_s[h].shape, 0)
        qry_idx = lax.broadcasted_iota(jnp.int32, own_s[h].shape, 1)
        st = jnp.where(key_idx <= qry_idx, own_s[h], NEG)
        m0 = _reduce_rows(st, jnp.maximum, jnp.max)
        p = jnp.exp2(st - m0)
        p_ref[0][h][...] = p.astype(bf16)
        acc_ref[h][...] = jnp.zeros(acc_ref[h].shape, f32)
        state.append((m0, _reduce_rows(p, jnp.add, jnp.sum), jnp.ones_like(m0)))

    def stage(j, slot, state):
        other = 1 - slot
        qk(jnp.minimum(j + 1, n_blocks - 1), other)
        pvs = pv(jnp.where(j == 0, i, j - 1), slot)
        new = []
        for h in heads:
            m, l, _ = state[h]
            s = s_ref[slot][h][...]
            on = sel_ref[h, pl.ds(j, 1), :] > 0.0
            m_new = jnp.maximum(m, jnp.where(on, _reduce_rows(s, jnp.maximum, jnp.max), NEG))
            alpha = jnp.exp2(m - m_new)
            p = jnp.exp2(s - jnp.where(on, m_new, -NEG))
            p_ref[other][h][...] = p.astype(bf16)
            new.append((m_new, alpha * l + _reduce_rows(p, jnp.add, jnp.sum), alpha))
        accumulate(pvs, [st[2] for st in state])
        return new

    def body(t, carry):
        state = [tuple(carry[3 * h:3 * h + 3]) for h in heads]
        state = stage(2 * t, 0, state)
        state = stage(2 * t + 1, 1, state)
        return tuple(x for st in state for x in st)

    n_pairs = (i + 1) // 2
    carry = lax.fori_loop(0, n_pairs, body, tuple(x for st in state for x in st))
    state = [tuple(carry[3 * h:3 * h + 3]) for h in heads]
    accumulate(pv(jnp.where(i == 0, 0, 2 * n_pairs - 1), 0), [st[2] for st in state])
    for h in heads:
        o_ref[:, hsl[h]] = (acc_ref[h][...] / state[h][1]).T.astype(o_ref.dtype)


def _moba_attention(proj3):
    b, s, _ = proj3.shape
    n_blocks = s // MOBA_BLOCK
    width = ATTN_HEADS * HEAD_DIM
    head_groups = N_HEADS // ATTN_HEADS
    return pl.pallas_call(
        functools.partial(_moba_kernel, n_blocks=n_blocks),
        out_shape=jax.ShapeDtypeStruct((b, s, N_HEADS * HEAD_DIM), jnp.bfloat16),
        grid=(b, head_groups, n_blocks),
        in_specs=[
            pl.BlockSpec((None, MOBA_BLOCK, width), lambda bi, h, i: (bi, i, h)),
            pl.BlockSpec((None, s, width), lambda bi, h, i: (bi, 0, head_groups + h)),
            pl.BlockSpec((None, s, width), lambda bi, h, i: (bi, 0, 2 * head_groups + h)),
        ],
        out_specs=pl.BlockSpec((None, MOBA_BLOCK, width), lambda bi, h, i: (bi, i, h)),
        scratch_shapes=[
            pltpu.VMEM((ATTN_HEADS, HEAD_DIM, s), jnp.bfloat16),
            pltpu.VMEM((ATTN_HEADS, n_blocks, HEAD_DIM), jnp.float32),
            pltpu.VMEM((ATTN_HEADS, n_blocks, MOBA_BLOCK), jnp.float32),
            pltpu.VMEM((ATTN_HEADS, HEAD_DIM, MOBA_BLOCK), jnp.bfloat16),
        ] + [pltpu.VMEM((MOBA_BLOCK, MOBA_BLOCK), jnp.float32)] * (2 * ATTN_HEADS)
          + [pltpu.VMEM((MOBA_BLOCK, MOBA_BLOCK), jnp.bfloat16)] * (2 * ATTN_HEADS)
          + [pltpu.VMEM((HEAD_DIM, MOBA_BLOCK), jnp.float32)] * ATTN_HEADS,
        compiler_params=pltpu.CompilerParams(
            dimension_semantics=("arbitrary", "arbitrary", "arbitrary"), vmem_limit_bytes=VMEM_LIMIT_BYTES),
        name="moba_attn",
    )(proj3, proj3, proj3)


def _mix_kernel(x_ref, attn_ref, cg_ref, bg_ref, xc_ref, ga_ref, gc_ref, cgh_ref, xch_ref,
                cw_ref, wab_ref, wcb_ref, wout_ref, o_ref, *, tiles_per_seq):
    f32 = jnp.float32
    i = pl.program_id(0)
    cx = cg_ref[...].astype(f32) * xc_ref[...].astype(f32)
    live = (i % tiles_per_seq != 0).astype(f32)
    halo = cgh_ref[...].astype(f32) * xch_ref[...].astype(f32) * live
    row = lax.broadcasted_iota(jnp.int32, cx.shape, 0)
    cw = cw_ref[...]
    z = cw[CONV_K - 1:CONV_K, :] * cx
    for d in range(1, CONV_K):
        shifted = pltpu.roll(cx, d, 0)
        for r in range(d):
            shifted = jnp.where(row == r, halo[HALO - d + r:HALO - d + r + 1, :], shifted)
        z = z + cw[CONV_K - 1 - d:CONV_K - d, :] * shifted
    yc = jnp.dot((bg_ref[...].astype(f32) * z).astype(jnp.bfloat16), wcb_ref[...], preferred_element_type=f32)
    ya = jnp.dot(attn_ref[...], wab_ref[...], preferred_element_type=f32)
    merged = jax.nn.sigmoid(ga_ref[...].astype(f32)) * ya + jax.nn.sigmoid(gc_ref[...].astype(f32)) * yc
    o_ref[...] = x_ref[...] + jnp.dot(merged.astype(jnp.bfloat16), wout_ref[...], preferred_element_type=f32)


def _mix(x2, attn2, proj, conv_w, wab, wcb, wout, seq):
    t = x2.shape[0]
    tiles_per_seq = seq // MIX_TM
    halo_per_tile = MIX_TM // HALO

    def col(c):
        return pl.BlockSpec((MIX_TM, D_MODEL), lambda i: (i, c))

    def halo(c):
        return pl.BlockSpec((HALO, D_MODEL), lambda i: (jnp.maximum(i * halo_per_tile - 1, 0), c))

    wspec = pl.BlockSpec((D_MODEL, D_MODEL), lambda i: (0, 0))
    return pl.pallas_call(
        functools.partial(_mix_kernel, tiles_per_seq=tiles_per_seq),
        out_shape=jax.ShapeDtypeStruct((t, D_MODEL), jnp.float32),
        grid=(t // MIX_TM,),
        in_specs=[
            pl.BlockSpec((MIX_TM, D_MODEL), lambda i: (i, 0)),
            pl.BlockSpec((MIX_TM, D_MODEL), lambda i: (i, 0)),
            col(3), col(4), col(5), col(6), col(7),
            halo(3), halo(5),
            pl.BlockSpec((CONV_K, D_MODEL), lambda i: (0, 0)),
            wspec, wspec, wspec,
        ],
        out_specs=pl.BlockSpec((MIX_TM, D_MODEL), lambda i: (i, 0)),
        compiler_params=pltpu.CompilerParams(
            dimension_semantics=("arbitrary",), vmem_limit_bytes=VMEM_LIMIT_BYTES),
        name="mix",
    )(x2, attn2, proj, proj, proj, proj, proj, proj, proj, conv_w, wab, wcb, wout)


def _ffn_kernel(h_ref, g_ref, wgu_ref, wd_ref, gf_ref, o_ref):
    f32 = jnp.float32
    h = h_ref[...]
    u = _rms(h, g_ref[...]).astype(jnp.bfloat16)
    ck = D_FF // FFN_CHUNKS
    acc = h
    for c in range(FFN_CHUNKS):
        gate = jnp.dot(u, wgu_ref[:, c * ck:(c + 1) * ck], preferred_element_type=f32)
        up = jnp.dot(u, wgu_ref[:, D_FF + c * ck:D_FF + (c + 1) * ck], preferred_element_type=f32)
        act = (gate * jax.nn.sigmoid(gate) * up).astype(jnp.bfloat16)
        acc = acc + jnp.dot(act, wd_ref[c * ck:(c + 1) * ck, :], preferred_element_type=f32)
    o_ref[...] = _rms(acc, gf_ref[...])


def _ffn(h1, g_ffn, wgu, wd, g_final):
    t = h1.shape[0]
    return pl.pallas_call(
        _ffn_kernel,
        out_shape=jax.ShapeDtypeStruct((t, D_MODEL), jnp.float32),
        grid=(t // FFN_TM,),
        in_specs=[
            pl.BlockSpec((FFN_TM, D_MODEL), lambda i: (i, 0)),
            pl.BlockSpec((1, D_MODEL), lambda i: (0, 0)),
            pl.BlockSpec((D_MODEL, 2 * D_FF), lambda i: (0, 0), pipeline_mode=pl.Buffered(1)),
            pl.BlockSpec((D_FF, D_MODEL), lambda i: (0, 0), pipeline_mode=pl.Buffered(1)),
            pl.BlockSpec((1, D_MODEL), lambda i: (0, 0)),
        ],
        out_specs=pl.BlockSpec((FFN_TM, D_MODEL), lambda i: (i, 0)),
        compiler_params=pltpu.CompilerParams(
            dimension_semantics=("arbitrary",), vmem_limit_bytes=VMEM_LIMIT_BYTES),
        name="ffn",
    )(h1, g_ffn, wgu, wd, g_final)


def _rope_tables(seq):
    inv_freq = ROPE_THETA ** (-jnp.arange(0, ROPE_DIM, 2, dtype=jnp.float32) / ROPE_DIM)
    ang = jnp.arange(seq, dtype=jnp.int32).astype(jnp.float32)[:, None] * inv_freq[None, :]
    cos, sin = jnp.cos(ang), jnp.sin(ang)
    rest = HEAD_DIM - ROPE_DIM
    cos_t = jnp.concatenate([cos, cos, jnp.ones((seq, rest), jnp.float32)], axis=1)
    zeros_h = jnp.zeros((seq, ROPE_HALF), jnp.float32)
    zeros_r = jnp.zeros((seq, rest), jnp.float32)
    sina = jnp.concatenate([-sin, zeros_h, zeros_r], axis=1)
    sinb = jnp.concatenate([zeros_h, sin, zeros_r], axis=1)
    return cos_t, sina, sinb


def kernel(x, g_mix, w_in, conv_w, w_attn_branch, w_conv_branch, w_out, g_ffn, w_gate_up, w_down, g_final):
    b, s, d = x.shape
    depth = w_in.shape[0]
    assert d == D_MODEL and s % PROJ_TM == 0 and s % MOBA_BLOCK == 0
    bf16 = jnp.bfloat16
    cos, sina, sinb = _rope_tables(s)
    h = x.reshape(b * s, d)
    for l in range(depth):
        proj = _in_proj(h, g_mix[l][None, :], w_in[l].astype(bf16), cos, sina, sinb, s)
        attn = _moba_attention(proj.reshape(b, s, IN_WIDTH))
        h = _mix(h, attn.reshape(b * s, d), proj, conv_w[l], w_attn_branch[l].astype(bf16),
                 w_conv_branch[l].astype(bf16), w_out[l].astype(bf16), s)
        last = l == depth - 1
        assert last, "only DEPTH == 1 is supported"
        h = _ffn(h, g_ffn[l][None, :], w_gate_up[l].astype(bf16), w_down[l].astype(bf16), g_final[None, :])
    return h.reshape(b, s, d)
```

```python
import functools
import math

import jax
import jax.numpy as jnp
import numpy as np
from jax import lax
from jax.experimental import pallas as pl
from jax.experimental.pallas import tpu as pltpu

D_MODEL = 1024
N_HEADS = 8
HEAD_DIM = 128
MOBA_BLOCK = 256
MOBA_TOPK = 3
ROPE_THETA = 500000.0
ROPE_DIM = HEAD_DIM // 4
ROPE_HALF = ROPE_DIM // 2
CONV_K = 3
D_FF = 2816
EPS = 1e-6
NEG = -1e30
N_SPLITS = 8
IN_WIDTH = N_SPLITS * D_MODEL

VMEM_LIMIT_BYTES = 56 * 1024 * 1024
PROJ_TM = 1024
MIX_TM = 512
FFN_TM = 512
FFN_CHUNKS = 2
ATTN_HEADS = 4
SUBLANES = 8
VT_ROWS = HEAD_DIM + 2 * SUBLANES
HALO = SUBLANES


def _rms(xf, g):
    return xf * lax.rsqrt(jnp.mean(xf * xf, axis=-1, keepdims=True) + EPS) * g


def _in_proj_kernel(x_ref, g_ref, w_ref, cos_ref, sina_ref, sinb_ref, o_ref, u_ref):
    j = pl.program_id(1)

    @pl.when(j == 0)
    def _():
        u_ref[...] = _rms(x_ref[...], g_ref[...]).astype(jnp.bfloat16)

    acc = jnp.dot(u_ref[...], w_ref[...], preferred_element_type=jnp.float32)

    def rope_store(scale):
        cos, sina, sinb = cos_ref[...], sina_ref[...], sinb_ref[...]
        for h in range(N_HEADS):
            t = acc[:, h * HEAD_DIM:(h + 1) * HEAD_DIM]
            r = (t * cos + pltpu.roll(t, HEAD_DIM - ROPE_HALF, 1) * sina
                 + pltpu.roll(t, ROPE_HALF, 1) * sinb)
            o_ref[:, h * HEAD_DIM:(h + 1) * HEAD_DIM] = (r * scale).astype(o_ref.dtype)

    @pl.when(j == 0)
    def _():
        rope_store(math.log2(math.e) / math.sqrt(HEAD_DIM))

    @pl.when(j == 1)
    def _():
        rope_store(1.0)

    @pl.when(j >= 2)
    def _():
        o_ref[...] = acc.astype(o_ref.dtype)


def _in_proj(x2, g_mix, w_in_bf16, cos, sina, sinb, seq):
    t = x2.shape[0]
    tiles_per_seq = seq // PROJ_TM
    tab_spec = pl.BlockSpec((PROJ_TM, HEAD_DIM), lambda i, j: (i % tiles_per_seq, 0))
    return pl.pallas_call(
        _in_proj_kernel,
        out_shape=jax.ShapeDtypeStruct((t, IN_WIDTH), jnp.bfloat16),
        grid=(t // PROJ_TM, N_SPLITS),
        in_specs=[
            pl.BlockSpec((PROJ_TM, D_MODEL), lambda i, j: (i, 0)),
            pl.BlockSpec((1, D_MODEL), lambda i, j: (0, 0)),
            pl.BlockSpec((D_MODEL, D_MODEL), lambda i, j: (0, j)),
            tab_spec, tab_spec, tab_spec,
        ],
        out_specs=pl.BlockSpec((PROJ_TM, D_MODEL), lambda i, j: (i, j)),
        scratch_shapes=[pltpu.VMEM((PROJ_TM, D_MODEL), jnp.bfloat16)],
        compiler_params=pltpu.CompilerParams(
            dimension_semantics=("arbitrary", "arbitrary"), vmem_limit_bytes=VMEM_LIMIT_BYTES),
        name="in_proj",
    )(x2, g_mix, w_in_bf16, cos, sina, sinb)


def _reduce_rows(x, op, reduce_fn):
    rows = x.shape[0]
    while rows > SUBLANES and rows % (2 * SUBLANES) == 0:
        rows //= 2
        x = op(x[:rows], x[rows:])
    return reduce_fn(x, axis=0, keepdims=True)


def _moba_kernel(q_ref, k_ref, v_ref, o_ref, vt_ref, kmean_ref, sel_ref, qt_ref, *slot_refs, n_blocks):
    i = pl.program_id(2)
    f32, bf16 = jnp.float32, jnp.bfloat16
    heads = range(ATTN_HEADS)
    hsl = [slice(h * HEAD_DIM, (h + 1) * HEAD_DIM) for h in heads]
    nh = ATTN_HEADS
    s_ref = [slot_refs[0:nh], slot_refs[nh:2 * nh]]
    p_ref = [slot_refs[2 * nh:3 * nh], slot_refs[3 * nh:4 * nh]]
    acc_ref = slot_refs[4 * nh:5 * nh]

    @pl.when(i == 0)
    def _():
        row = lax.broadcasted_iota(jnp.int32, (VT_ROWS - HEAD_DIM, vt_ref.shape[2]), 0)
        for h in heads:
            vt_ref[h, HEAD_DIM:, :] = jnp.where(row == 0, 1.0, 0.0).astype(bf16)
            for c in range(n_blocks):
                rows = slice(c * MOBA_BLOCK, (c + 1) * MOBA_BLOCK)
                kmean_ref[h, c:c + 1, :] = jnp.mean(k_ref[rows, hsl[h]].astype(f32), axis=0, keepdims=True)
                vt_ref[h, :HEAD_DIM, rows] = v_ref[rows, hsl[h]].astype(f32).T.astype(bf16)

    def krows(blk):
        return pl.ds(pl.multiple_of(blk * MOBA_BLOCK, MOBA_BLOCK), MOBA_BLOCK)

    def qk(blk, slot):
        for h in heads:
            s_ref[slot][h][...] = jnp.dot(k_ref[krows(blk), hsl[h]], qt_ref[h], preferred_element_type=f32)

    def pv(blk, slot):
        return [jnp.dot(vt_ref[h, :, krows(blk)], p_ref[slot][h][...], preferred_element_type=f32) for h in heads]

    def accumulate(pvs, alphas):
        for h in heads:
            acc_ref[h][...] = alphas[h] * acc_ref[h][...] + pvs[h]

    for h in heads:
        qt_ref[h] = q_ref[:, hsl[h]].astype(f32).T.astype(bf16)
    gates = [jnp.dot(kmean_ref[h].astype(bf16), qt_ref[h], preferred_element_type=f32) for h in heads]
    own_s = [jnp.dot(k_ref[krows(i), hsl[h]], qt_ref[h], preferred_element_type=f32) for h in heads]
    qk(0, 0)

    state = []
    for h in heads:
        blk = lax.broadcasted_iota(jnp.int32, gates[h].shape, 0)
        past = blk < i
        g = jnp.where(past, gates[h], NEG)
        sel = jnp.zeros(g.shape, f32)
        for _ in range(MOBA_TOPK):
            mx = jnp.max(g, axis=0, keepdims=True)
            first = jnp.min(jnp.where(g == mx, blk, n_blocks), axis=0, keepdims=True)
            pick = blk == first
            sel = jnp.where(pick, 1.0, sel)
            g = jnp.where(pick, -jnp.inf, g)
        sel_ref[h] = jnp.where(past, sel, 0.0)

        key_idx = lax.broadcasted_iota(jnp.int32, own_s[h].shape, 0)
        qry_idx = lax.broadcasted_iota(jnp.int32, own_s[h].shape, 1)
        st = jnp.where(key_idx <= qry_idx, own_s[h], NEG)
        m0 = _reduce_rows(st, jnp.maximum, jnp.max)
        p_ref[0][h][...] = jnp.exp2((st - m0).astype(bf16))
        p_ref[1][h][...] = jnp.zeros(p_ref[1][h].shape, bf16)
        acc_ref[h][...] = jnp.zeros(acc_ref[h].shape, f32)
        state.append((m0, jnp.ones_like(m0), jnp.ones_like(m0)))

    def stage(j, slot, pv_blk, state):
        qk(jnp.minimum(j + 1, n_blocks - 1), 1 - slot)
        pvs = pv(pv_blk, slot)
        new = []
        for h in heads:
            m, a1, _ = state[h]
            s = s_ref[slot][h][...]
            on = sel_ref[h, pl.ds(j, 1), :] > 0.0
            m_new = jnp.maximum(m, jnp.where(on, _reduce_rows(s, jnp.maximum, jnp.max), NEG))
            p_ref[slot][h][...] = jnp.exp2((s - jnp.where(on, m_new, -NEG)).astype(bf16))
            new.append((m_new, jnp.exp2(m - m_new), a1))
        accumulate(pvs, [st[2] for st in state])
        return new

    n_state = 3

    def body(t, carry):
        state = [tuple(carry[n_state * h:n_state * (h + 1)]) for h in heads]
        first = t == 0
        state = stage(2 * t, 0, jnp.where(first, i, 2 * t - 2), state)
        state = stage(2 * t + 1, 1, jnp.where(first, 0, 2 * t - 1), state)
        return tuple(x for st in state for x in st)

    n_pairs = (i + 1) // 2
    carry = lax.fori_loop(0, n_pairs, body, tuple(x for st in state for x in st))
    state = [tuple(carry[n_state * h:n_state * (h + 1)]) for h in heads]
    none = n_pairs == 0
    accumulate(pv(jnp.where(none, i, 2 * n_pairs - 2), 0), [st[2] for st in state])
    accumulate(pv(jnp.where(none, 0, 2 * n_pairs - 1), 1), [st[1] for st in state])
    for h in heads:
        acc = acc_ref[h][...]
        out = acc[:HEAD_DIM] / acc[HEAD_DIM:HEAD_DIM + 1]
        o_ref[:, hsl[h]] = out.T.astype(o_ref.dtype)


def _moba_attention(proj3):
    b, s, _ = proj3.shape
    n_blocks = s // MOBA_BLOCK
    width = ATTN_HEADS * HEAD_DIM
    head_groups = N_HEADS // ATTN_HEADS
    return pl.pallas_call(
        functools.partial(_moba_kernel, n_blocks=n_blocks),
        out_shape=jax.ShapeDtypeStruct((b, s, N_HEADS * HEAD_DIM), jnp.bfloat16),
        grid=(b, head_groups, n_blocks),
        in_specs=[
            pl.BlockSpec((None, MOBA_BLOCK, width), lambda bi, h, i: (bi, i, h)),
            pl.BlockSpec((None, s, width), lambda bi, h, i: (bi, 0, head_groups + h)),
            pl.BlockSpec((None, s, width), lambda bi, h, i: (bi, 0, 2 * head_groups + h)),
        ],
        out_specs=pl.BlockSpec((None, MOBA_BLOCK, width), lambda bi, h, i: (bi, i, h)),
        scratch_shapes=[
            pltpu.VMEM((ATTN_HEADS, VT_ROWS, s), jnp.bfloat16),
            pltpu.VMEM((ATTN_HEADS, n_blocks, HEAD_DIM), jnp.float32),
            pltpu.VMEM((ATTN_HEADS, n_blocks, MOBA_BLOCK), jnp.float32),
            pltpu.VMEM((ATTN_HEADS, HEAD_DIM, MOBA_BLOCK), jnp.bfloat16),
        ] + [pltpu.VMEM((MOBA_BLOCK, MOBA_BLOCK), jnp.float32)] * (2 * ATTN_HEADS)
          + [pltpu.VMEM((MOBA_BLOCK, MOBA_BLOCK), jnp.bfloat16)] * (2 * ATTN_HEADS)
          + [pltpu.VMEM((VT_ROWS, MOBA_BLOCK), jnp.float32)] * ATTN_HEADS,
        compiler_params=pltpu.CompilerParams(
            dimension_semantics=("arbitrary", "arbitrary", "arbitrary"), vmem_limit_bytes=VMEM_LIMIT_BYTES),
        name="moba_attn",
    )(proj3, proj3, proj3)


def _mix_kernel(x_ref, attn_ref, cg_ref, bg_ref, xc_ref, ga_ref, gc_ref, cgh_ref, xch_ref,
                cw_ref, wab_ref, wcb_ref, wout_ref, o_ref, *, tiles_per_seq):
    f32 = jnp.float32
    i = pl.program_id(0)
    cx = cg_ref[...].astype(f32) * xc_ref[...].astype(f32)
    live = (i % tiles_per_seq != 0).astype(f32)
    halo = cgh_ref[...].astype(f32) * xch_ref[...].astype(f32) * live
    row = lax.broadcasted_iota(jnp.int32, cx.shape, 0)
    cw = cw_ref[...]
    z = cw[CONV_K - 1:CONV_K, :] * cx
    for d in range(1, CONV_K):
        shifted = pltpu.roll(cx, d, 0)
        for r in range(d):
            shifted = jnp.where(row == r, halo[HALO - d + r:HALO - d + r + 1, :], shifted)
        z = z + cw[CONV_K - 1 - d:CONV_K - d, :] * shifted
    yc = jnp.dot((bg_ref[...].astype(f32) * z).astype(jnp.bfloat16), wcb_ref[...], preferred_element_type=f32)
    ya = jnp.dot(attn_ref[...], wab_ref[...], preferred_element_type=f32)
    merged = jax.nn.sigmoid(ga_ref[...].astype(f32)) * ya + jax.nn.sigmoid(gc_ref[...].astype(f32)) * yc
    o_ref[...] = x_ref[...] + jnp.dot(merged.astype(jnp.bfloat16), wout_ref[...], preferred_element_type=f32)


def _mix(x2, attn2, proj, conv_w, wab, wcb, wout, seq):
    t = x2.shape[0]
    tiles_per_seq = seq // MIX_TM
    halo_per_tile = MIX_TM // HALO

    def col(c):
        return pl.BlockSpec((MIX_TM, D_MODEL), lambda i: (i, c))

    def halo(c):
        return pl.BlockSpec((HALO, D_MODEL), lambda i: (jnp.maximum(i * halo_per_tile - 1, 0), c))

    wspec = pl.BlockSpec((D_MODEL, D_MODEL), lambda i: (0, 0))
    return pl.pallas_call(
        functools.partial(_mix_kernel, tiles_per_seq=tiles_per_seq),
        out_shape=jax.ShapeDtypeStruct((t, D_MODEL), jnp.float32),
        grid=(t // MIX_TM,),
        in_specs=[
            pl.BlockSpec((MIX_TM, D_MODEL), lambda i: (i, 0)),
            pl.BlockSpec((MIX_TM, D_MODEL), lambda i: (i, 0)),
            col(3), col(4), col(5), col(6), col(7),
            halo(3), halo(5),
            pl.BlockSpec((CONV_K, D_MODEL), lambda i: (0, 0)),
            wspec, wspec, wspec,
        ],
        out_specs=pl.BlockSpec((MIX_TM, D_MODEL), lambda i: (i, 0)),
        compiler_params=pltpu.CompilerParams(
            dimension_semantics=("arbitrary",), vmem_limit_bytes=VMEM_LIMIT_BYTES),
        name="mix",
    )(x2, attn2, proj, proj, proj, proj, proj, proj, proj, conv_w, wab, wcb, wout)


def _ffn_kernel(h_ref, g_ref, wgu_ref, wd_ref, gf_ref, o_ref):
    f32 = jnp.float32
    h = h_ref[...]
    u = _rms(h, g_ref[...]).astype(jnp.bfloat16)
    ck = D_FF // FFN_CHUNKS
    acc = h
    for c in range(FFN_CHUNKS):
        gate = jnp.dot(u, wgu_ref[:, c * ck:(c + 1) * ck], preferred_element_type=f32)
        up = jnp.dot(u, wgu_ref[:, D_FF + c * ck:D_FF + (c + 1) * ck], preferred_element_type=f32)
        act = (gate * jax.nn.sigmoid(gate) * up).astype(jnp.bfloat16)
        acc = acc + jnp.dot(act, wd_ref[c * ck:(c + 1) * ck, :], preferred_element_type=f32)
    o_ref[...] = _rms(acc, gf_ref[...])


def _ffn(h1, g_ffn, wgu, wd, g_final):
    t = h1.shape[0]
    return pl.pallas_call(
        _ffn_kernel,
        out_shape=jax.ShapeDtypeStruct((t, D_MODEL), jnp.float32),
        grid=(t // FFN_TM,),
        in_specs=[
            pl.BlockSpec((FFN_TM, D_MODEL), lambda i: (i, 0)),
            pl.BlockSpec((1, D_MODEL), lambda i: (0, 0)),
            pl.BlockSpec((D_MODEL, 2 * D_FF), lambda i: (0, 0), pipeline_mode=pl.Buffered(1)),
            pl.BlockSpec((D_FF, D_MODEL), lambda i: (0, 0), pipeline_mode=pl.Buffered(1)),
            pl.BlockSpec((1, D_MODEL), lambda i: (0, 0)),
        ],
        out_specs=pl.BlockSpec((FFN_TM, D_MODEL), lambda i: (i, 0)),
        compiler_params=pltpu.CompilerParams(
            dimension_semantics=("arbitrary",), vmem_limit_bytes=VMEM_LIMIT_BYTES),
        name="ffn",
    )(h1, g_ffn, wgu, wd, g_final)


def _rope_tables(seq):
    inv_freq = ROPE_THETA ** (-jnp.arange(0, ROPE_DIM, 2, dtype=jnp.float32) / ROPE_DIM)
    ang = jnp.arange(seq, dtype=jnp.int32).astype(jnp.float32)[:, None] * inv_freq[None, :]
    cos, sin = jnp.cos(ang), jnp.sin(ang)
    rest = HEAD_DIM - ROPE_DIM
    cos_t = jnp.concatenate([cos, cos, jnp.ones((seq, rest), jnp.float32)], axis=1)
    zeros_h = jnp.zeros((seq, ROPE_HALF), jnp.float32)
    zeros_r = jnp.zeros((seq, rest), jnp.float32)
    sina = jnp.concatenate([-sin, zeros_h, zeros_r], axis=1)
    sinb = jnp.concatenate([zeros_h, sin, zeros_r], axis=1)
    return cos_t, sina, sinb


def kernel(x, g_mix, w_in, conv_w, w_attn_branch, w_conv_branch, w_out, g_ffn, w_gate_up, w_down, g_final):
    b, s, d = x.shape
    depth = w_in.shape[0]
    assert d == D_MODEL and s % PROJ_TM == 0 and s % MOBA_BLOCK == 0
    bf16 = jnp.bfloat16
    cos, sina, sinb = _rope_tables(s)
    h = x.reshape(b * s, d)
    for l in range(depth):
        proj = _in_proj(h, g_mix[l][None, :], w_in[l].astype(bf16), cos, sina, sinb, s)
        attn = _moba_attention(proj.reshape(b, s, IN_WIDTH))
        h = _mix(h, attn.reshape(b * s, d), proj, conv_w[l], w_attn_branch[l].astype(bf16),
                 w_conv_branch[l].astype(bf16), w_out[l].astype(bf16), s)
        last = l == depth - 1
        assert last, "only DEPTH == 1 is supported"
        h = _ffn(h, g_ffn[l][None, :], w_gate_up[l].astype(bf16), w_down[l].astype(bf16), g_final[None, :])
    return h.reshape(b, s, d)
```

```python
import functools
import math

import jax
import jax.numpy as jnp
import numpy as np
from jax import lax
from jax.experimental import pallas as pl
from jax.experimental.pallas import tpu as pltpu

D_MODEL = 1024
N_HEADS = 8
HEAD_DIM = 128
MOBA_BLOCK = 256
MOBA_TOPK = 3
ROPE_THETA = 500000.0
ROPE_DIM = HEAD_DIM // 4
ROPE_HALF = ROPE_DIM // 2
CONV_K = 3
D_FF = 2816
EPS = 1e-6
NEG = -1e30
N_SPLITS = 8
IN_WIDTH = N_SPLITS * D_MODEL

VMEM_LIMIT_BYTES = 56 * 1024 * 1024
PROJ_TM = 1024
PROJ_SUB = 256
MIX_TM = 512
FFN_TM = 512
FFN_CHUNKS = 1
ATTN_HEADS = 4
ATTN_SLOTS = 4
SUBLANES = 8
VT_ROWS = HEAD_DIM + 2 * SUBLANES
HALO = SUBLANES


def _rms(xf, g):
    return xf * lax.rsqrt(jnp.mean(xf * xf, axis=-1, keepdims=True) + EPS) * g


def _in_proj_kernel(x_ref, g_ref, w_ref, cos_ref, sina_ref, sinb_ref, o_ref, u_ref):
    j = pl.program_id(1)

    @pl.when(j == 0)
    def _():
        u_ref[...] = _rms(x_ref[...], g_ref[...]).astype(jnp.bfloat16)

    def pieces(store):
        for c in range(D_MODEL // PROJ_SUB):
            cols = slice(c * PROJ_SUB, (c + 1) * PROJ_SUB)
            store(jnp.dot(u_ref[...], w_ref[:, cols], preferred_element_type=jnp.float32), c * PROJ_SUB)

    def rope_store(acc, col0):
        cos, sina, sinb = cos_ref[...], sina_ref[...], sinb_ref[...]
        for h in range(PROJ_SUB // HEAD_DIM):
            t = acc[:, h * HEAD_DIM:(h + 1) * HEAD_DIM]
            r = (t * cos + pltpu.roll(t, HEAD_DIM - ROPE_HALF, 1) * sina
                 + pltpu.roll(t, ROPE_HALF, 1) * sinb)
            o_ref[:, col0 + h * HEAD_DIM:col0 + (h + 1) * HEAD_DIM] = r.astype(o_ref.dtype)

    def plain_store(acc, col0):
        o_ref[:, col0:col0 + PROJ_SUB] = acc.astype(o_ref.dtype)

    @pl.when(j < 2)
    def _():
        pieces(rope_store)

    @pl.when(j >= 2)
    def _():
        pieces(plain_store)


def _in_proj(x2, g_mix, w_in_bf16, cos, sina, sinb, seq):
    t = x2.shape[0]
    tiles_per_seq = seq // PROJ_TM
    tab_spec = pl.BlockSpec((None, PROJ_TM, HEAD_DIM), lambda i, j: (jnp.minimum(j, 1), i % tiles_per_seq, 0))
    return pl.pallas_call(
        _in_proj_kernel,
        out_shape=jax.ShapeDtypeStruct((t, IN_WIDTH), jnp.bfloat16),
        grid=(t // PROJ_TM, N_SPLITS),
        in_specs=[
            pl.BlockSpec((PROJ_TM, D_MODEL), lambda i, j: (i, 0)),
            pl.BlockSpec((1, D_MODEL), lambda i, j: (0, 0)),
            pl.BlockSpec((D_MODEL, D_MODEL), lambda i, j: (0, j)),
            tab_spec, tab_spec, tab_spec,
        ],
        out_specs=pl.BlockSpec((PROJ_TM, D_MODEL), lambda i, j: (i, j)),
        scratch_shapes=[pltpu.VMEM((PROJ_TM, D_MODEL), jnp.bfloat16)],
        compiler_params=pltpu.CompilerParams(
            dimension_semantics=("arbitrary", "arbitrary"), vmem_limit_bytes=VMEM_LIMIT_BYTES),
        name="in_proj",
    )(x2, g_mix, w_in_bf16, cos, sina, sinb)


def _reduce_rows(x, op, reduce_fn):
    rows = x.shape[0]
    while rows > SUBLANES and rows % (2 * SUBLANES) == 0:
        rows //= 2
        x = op(x[:rows], x[rows:])
    return reduce_fn(x, axis=0, keepdims=True)


def _moba_kernel(q_ref, k_ref, v_ref, o_ref, vt_ref, kmean_ref, sel_ref, qt_ref, stat_ref, *slot_refs, n_blocks):
    i = pl.program_id(2)
    f32, bf16 = jnp.float32, jnp.bfloat16
    heads = range(ATTN_HEADS)
    hsl = [slice(h * HEAD_DIM, (h + 1) * HEAD_DIM) for h in heads]
    nh = ATTN_HEADS
    ns = ATTN_SLOTS
    groups = [slot_refs[g * nh:(g + 1) * nh] for g in range(len(slot_refs) // nh)]
    s_ref = groups[0:ns]
    smax_ref = groups[ns:2 * ns]
    p_ref = groups[2 * ns:3 * ns]
    acc_ref = groups[3 * ns]

    @pl.when(i == 0)
    def _():
        row = lax.broadcasted_iota(jnp.int32, (VT_ROWS - HEAD_DIM, vt_ref.shape[2]), 0)
        for h in heads:
            vt_ref[h, HEAD_DIM:, :] = jnp.where(row == 0, 1.0, 0.0).astype(bf16)
            for c in range(n_blocks):
                rows = slice(c * MOBA_BLOCK, (c + 1) * MOBA_BLOCK)
                kmean_ref[h, c:c + 1, :] = jnp.mean(k_ref[rows, hsl[h]].astype(f32), axis=0, keepdims=True)
                vt_ref[h, :HEAD_DIM, rows] = v_ref[rows, hsl[h]].astype(f32).T.astype(bf16)

    def krows(blk):
        return pl.ds(pl.multiple_of(blk * MOBA_BLOCK, MOBA_BLOCK), MOBA_BLOCK)

    def qk(blk, slot):
        for h in heads:
            s = jnp.dot(k_ref[krows(blk), hsl[h]], qt_ref[h], preferred_element_type=f32)
            s_ref[slot][h][...] = s
            smax_ref[slot][h][...] = _reduce_rows(s, jnp.maximum, jnp.max)

    def pv(blk, slot):
        return [jnp.dot(vt_ref[h, :, krows(blk)], p_ref[slot][h][...], preferred_element_type=f32) for h in heads]

    def accumulate(pvs, alphas):
        for h in heads:
            acc_ref[h][...] = alphas[h] * acc_ref[h][...] + pvs[h]

    for h in heads:
        qt_ref[h] = q_ref[:, hsl[h]].astype(f32).T.astype(bf16)
    gates = [jnp.dot(kmean_ref[h].astype(bf16), qt_ref[h], preferred_element_type=f32) for h in heads]
    own_s = [jnp.dot(k_ref[krows(i), hsl[h]], qt_ref[h], preferred_element_type=f32) for h in heads]
    qk(0, 0)

    for h in heads:
        blk = lax.broadcasted_iota(jnp.int32, gates[h].shape, 0)
        past = blk < i
        g = jnp.where(past, gates[h], NEG)
        sel = jnp.zeros(g.shape, f32)
        for _ in range(MOBA_TOPK):
            mx = jnp.max(g, axis=0, keepdims=True)
            first = jnp.min(jnp.where(g == mx, blk, n_blocks), axis=0, keepdims=True)
            pick = blk == first
            sel = jnp.where(pick, 1.0, sel)
            g = jnp.where(pick, -jnp.inf, g)
        sel_ref[h] = jnp.where(past, sel, 0.0)

        key_idx = lax.broadcasted_iota(jnp.int32, own_s[h].shape, 0)
        qry_idx = lax.broadcasted_iota(jnp.int32, own_s[h].shape, 1)
        st = jnp.where(key_idx <= qry_idx, own_s[h], NEG)
        m0 = _reduce_rows(st, jnp.maximum, jnp.max)
        p_ref[ns - 2][h][...] = jnp.exp2((st - m0).astype(bf16))
        p_ref[ns - 1][h][...] = jnp.zeros(p_ref[ns - 1][h].shape, bf16)
        acc_ref[h][...] = jnp.zeros(acc_ref[h].shape, f32)
        stat_ref[h, 0], stat_ref[h, 1], stat_ref[h, 2] = m0, jnp.ones_like(m0), jnp.ones_like(m0)

    def pending_block(j):
        return jnp.where(j == 0, i, jnp.maximum(j - 2, 0))

    def stage(j, r):
        qk(jnp.minimum(j + 1, n_blocks - 1), (r + 1) % ns)
        pvs = pv(pending_block(j), (r - 2) % ns)
        old = [(stat_ref[h, 0], stat_ref[h, 1], stat_ref[h, 2]) for h in heads]
        for h in heads:
            m, a1, _ = old[h]
            on = sel_ref[h, pl.ds(j, 1), :] > 0.0
            m_new = jnp.maximum(m, jnp.where(on, smax_ref[r][h][...], NEG))
            shift = jnp.where(on, m_new, -NEG)
            p_ref[r][h][...] = jnp.exp2((s_ref[r][h][...] - shift).astype(bf16))
            stat_ref[h, 0], stat_ref[h, 1], stat_ref[h, 2] = m_new, jnp.exp2(m - m_new), a1
        accumulate(pvs, [st[2] for st in old])

    def trip(t, carry):
        for r in range(ns):
            stage(ns * t + r, r)
        return carry

    n_trips = i // ns
    done = n_trips * ns
    rem = i - done
    lax.fori_loop(0, n_trips, trip, 0)
    for half in range(ns // 2):
        @pl.when(rem > 2 * half)
        def _():
            stage(done + 2 * half, 2 * half)
            stage(done + 2 * half + 1, 2 * half + 1)
    n_done = done + 2 * ((rem + 1) // 2)

    def drain(slot):
        accumulate(pv(pending_block(n_done), slot), [stat_ref[h, 2] for h in heads])
        accumulate(pv(pending_block(n_done + 1), slot + 1), [stat_ref[h, 1] for h in heads])

    for half in range(ns // 2):
        @pl.when(n_done % ns == (2 * half + 2) % ns)
        def _():
            drain(2 * half)
    for h in heads:
        acc = acc_ref[h][...]
        out = acc[:HEAD_DIM] / acc[HEAD_DIM:HEAD_DIM + 1]
        o_ref[:, hsl[h]] = out.T.astype(o_ref.dtype)


def _moba_attention(proj3):
    b, s, _ = proj3.shape
    n_blocks = s // MOBA_BLOCK
    width = ATTN_HEADS * HEAD_DIM
    head_groups = N_HEADS // ATTN_HEADS
    return pl.pallas_call(
        functools.partial(_moba_kernel, n_blocks=n_blocks),
        out_shape=jax.ShapeDtypeStruct((b, s, N_HEADS * HEAD_DIM), jnp.bfloat16),
        grid=(b, head_groups, n_blocks),
        in_specs=[
            pl.BlockSpec((None, MOBA_BLOCK, width), lambda bi, h, i: (bi, i, h)),
            pl.BlockSpec((None, s, width), lambda bi, h, i: (bi, 0, head_groups + h)),
            pl.BlockSpec((None, s, width), lambda bi, h, i: (bi, 0, 2 * head_groups + h)),
        ],
        out_specs=pl.BlockSpec((None, MOBA_BLOCK, width), lambda bi, h, i: (bi, i, h)),
        scratch_shapes=[
            pltpu.VMEM((ATTN_HEADS, VT_ROWS, s), jnp.bfloat16),
            pltpu.VMEM((ATTN_HEADS, n_blocks, HEAD_DIM), jnp.float32),
            pltpu.VMEM((ATTN_HEADS, n_blocks, MOBA_BLOCK), jnp.float32),
            pltpu.VMEM((ATTN_HEADS, HEAD_DIM, MOBA_BLOCK), jnp.bfloat16),
            pltpu.VMEM((ATTN_HEADS, 3, 1, MOBA_BLOCK), jnp.float32),
        ] + [pltpu.VMEM((MOBA_BLOCK, MOBA_BLOCK), jnp.float32)] * (ATTN_SLOTS * ATTN_HEADS)
          + [pltpu.VMEM((1, MOBA_BLOCK), jnp.float32)] * (ATTN_SLOTS * ATTN_HEADS)
          + [pltpu.VMEM((MOBA_BLOCK, MOBA_BLOCK), jnp.bfloat16)] * (ATTN_SLOTS * ATTN_HEADS)
          + [pltpu.VMEM((VT_ROWS, MOBA_BLOCK), jnp.float32)] * ATTN_HEADS,
        compiler_params=pltpu.CompilerParams(
            dimension_semantics=("arbitrary", "arbitrary", "arbitrary"), vmem_limit_bytes=VMEM_LIMIT_BYTES),
        name="moba_attn",
    )(proj3, proj3, proj3)


def _mix_kernel(x_ref, attn_ref, cg_ref, bg_ref, xc_ref, ga_ref, gc_ref, cgh_ref, xch_ref,
                cw_ref, wab_ref, wcb_ref, wout_ref, o_ref, *, tiles_per_seq):
    f32 = jnp.float32
    i = pl.program_id(0)
    cx = cg_ref[...].astype(f32) * xc_ref[...].astype(f32)
    live = (i % tiles_per_seq != 0).astype(f32)
    halo = cgh_ref[...].astype(f32) * xch_ref[...].astype(f32) * live
    row = lax.broadcasted_iota(jnp.int32, cx.shape, 0)
    cw = cw_ref[...]
    z = cw[CONV_K - 1:CONV_K, :] * cx
    for d in range(1, CONV_K):
        shifted = pltpu.roll(cx, d, 0)
        for r in range(d):
            shifted = jnp.where(row == r, halo[HALO - d + r:HALO - d + r + 1, :], shifted)
        z = z + cw[CONV_K - 1 - d:CONV_K - d, :] * shifted
    yc = jnp.dot((bg_ref[...].astype(f32) * z).astype(jnp.bfloat16), wcb_ref[...], preferred_element_type=f32)
    ya = jnp.dot(attn_ref[...], wab_ref[...], preferred_element_type=f32)
    merged = jax.nn.sigmoid(ga_ref[...].astype(f32)) * ya + jax.nn.sigmoid(gc_ref[...].astype(f32)) * yc
    o_ref[...] = x_ref[...] + jnp.dot(merged.astype(jnp.bfloat16), wout_ref[...], preferred_element_type=f32)


def _mix(x2, attn2, proj, conv_w, wab, wcb, wout, seq):
    t = x2.shape[0]
    tiles_per_seq = seq // MIX_TM
    halo_per_tile = MIX_TM // HALO

    def col(c):
        return pl.BlockSpec((MIX_TM, D_MODEL), lambda i: (i, c))

    def halo(c):
        return pl.BlockSpec((HALO, D_MODEL), lambda i: (jnp.maximum(i * halo_per_tile - 1, 0), c))

    wspec = pl.BlockSpec((D_MODEL, D_MODEL), lambda i: (0, 0))
    return pl.pallas_call(
        functools.partial(_mix_kernel, tiles_per_seq=tiles_per_seq),
        out_shape=jax.ShapeDtypeStruct((t, D_MODEL), jnp.float32),
        grid=(t // MIX_TM,),
        in_specs=[
            pl.BlockSpec((MIX_TM, D_MODEL), lambda i: (i, 0)),
            pl.BlockSpec((MIX_TM, D_MODEL), lambda i: (i, 0)),
            col(3), col(4), col(5), col(6), col(7),
            halo(3), halo(5),
            pl.BlockSpec((CONV_K, D_MODEL), lambda i: (0, 0)),
            wspec, wspec, wspec,
        ],
        out_specs=pl.BlockSpec((MIX_TM, D_MODEL), lambda i: (i, 0)),
        compiler_params=pltpu.CompilerParams(
            dimension_semantics=("arbitrary",), vmem_limit_bytes=VMEM_LIMIT_BYTES),
        name="mix",
    )(x2, attn2, proj, proj, proj, proj, proj, proj, proj, conv_w, wab, wcb, wout)


def _ffn_kernel(h_ref, g_ref, wgu_ref, wd_ref, gf_ref, o_ref):
    f32 = jnp.float32
    h = h_ref[...]
    u = _rms(h, g_ref[...]).astype(jnp.bfloat16)
    ck = D_FF // FFN_CHUNKS
    acc = h
    for c in range(FFN_CHUNKS):
        gate = jnp.dot(u, wgu_ref[:, c * ck:(c + 1) * ck], preferred_element_type=f32)
        up = jnp.dot(u, wgu_ref[:, D_FF + c * ck:D_FF + (c + 1) * ck], preferred_element_type=f32)
        act = (gate * jax.nn.sigmoid(gate) * up).astype(jnp.bfloat16)
        acc = acc + jnp.dot(act, wd_ref[c * ck:(c + 1) * ck, :], preferred_element_type=f32)
    o_ref[...] = _rms(acc, gf_ref[...])


def _ffn(h1, g_ffn, wgu, wd, g_final):
    t = h1.shape[0]
    return pl.pallas_call(
        _ffn_kernel,
        out_shape=jax.ShapeDtypeStruct((t, D_MODEL), jnp.float32),
        grid=(t // FFN_TM,),
        in_specs=[
            pl.BlockSpec((FFN_TM, D_MODEL), lambda i: (i, 0)),
            pl.BlockSpec((1, D_MODEL), lambda i: (0, 0)),
            pl.BlockSpec((D_MODEL, 2 * D_FF), lambda i: (0, 0), pipeline_mode=pl.Buffered(1)),
            pl.BlockSpec((D_FF, D_MODEL), lambda i: (0, 0), pipeline_mode=pl.Buffered(1)),
            pl.BlockSpec((1, D_MODEL), lambda i: (0, 0)),
        ],
        out_specs=pl.BlockSpec((FFN_TM, D_MODEL), lambda i: (i, 0)),
        compiler_params=pltpu.CompilerParams(
            dimension_semantics=("arbitrary",), vmem_limit_bytes=VMEM_LIMIT_BYTES),
        name="ffn",
    )(h1, g_ffn, wgu, wd, g_final)


def _rope_tables(seq):
    inv_freq = ROPE_THETA ** (-jnp.arange(0, ROPE_DIM, 2, dtype=jnp.float32) / ROPE_DIM)
    ang = jnp.arange(seq, dtype=jnp.int32).astype(jnp.float32)[:, None] * inv_freq[None, :]
    cos, sin = jnp.cos(ang), jnp.sin(ang)
    rest = HEAD_DIM - ROPE_DIM
    cos_t = jnp.concatenate([cos, cos, jnp.ones((seq, rest), jnp.float32)], axis=1)
    zeros_h = jnp.zeros((seq, ROPE_HALF), jnp.float32)
    zeros_r = jnp.zeros((seq, rest), jnp.float32)
    sina = jnp.concatenate([-sin, zeros_h, zeros_r], axis=1)
    sinb = jnp.concatenate([zeros_h, sin, zeros_r], axis=1)
    scales = jnp.array([math.log2(math.e) / math.sqrt(HEAD_DIM), 1.0], jnp.float32)[:, None, None]
    return tuple(scales * tab[None] for tab in (cos_t, sina, sinb))


def kernel(x, g_mix, w_in, conv_w, w_attn_branch, w_conv_branch, w_out, g_ffn, w_gate_up, w_down, g_final):
    b, s, d = x.shape
    depth = w_in.shape[0]
    assert d == D_MODEL and s % PROJ_TM == 0 and s % MOBA_BLOCK == 0
    bf16 = jnp.bfloat16
    cos, sina, sinb = _rope_tables(s)
    h = x.reshape(b * s, d)
    for l in range(depth):
        proj = _in_proj(h, g_mix[l][None, :], w_in[l].astype(bf16), cos, sina, sinb, s)
        attn = _moba_attention(proj.reshape(b, s, IN_WIDTH))
        h = _mix(h, attn.reshape(b * s, d), proj, conv_w[l], w_attn_branch[l].astype(bf16),
                 w_conv_branch[l].astype(bf16), w_out[l].astype(bf16), s)
        last = l == depth - 1
        assert last, "only DEPTH == 1 is supported"
        h = _ffn(h, g_ffn[l][None, :], w_gate_up[l].astype(bf16), w_down[l].astype(bf16), g_final[None, :])
    return h.reshape(b, s, d)
```

```python
import functools
import math

import jax
import jax.numpy as jnp
import numpy as np
from jax import lax
from jax.experimental import pallas as pl
from jax.experimental.pallas import tpu as pltpu

D_MODEL = 1024
N_HEADS = 8
HEAD_DIM = 128
MOBA_BLOCK = 256
MOBA_TOPK = 3
ROPE_THETA = 500000.0
ROPE_DIM = HEAD_DIM // 4
ROPE_HALF = ROPE_DIM // 2
CONV_K = 3
D_FF = 2816
EPS = 1e-6
NEG = -1e30
N_SPLITS = 8
IN_WIDTH = N_SPLITS * D_MODEL

VMEM_LIMIT_BYTES = 56 * 1024 * 1024
PROJ_TM = 2048
PROJ_SUB = 256
MIX_TM = 512
FFN_TM = 512
FFN_CHUNKS = 1
ATTN_HEADS = 4
ATTN_SLOTS = 4
SUBLANES = 8
VT_ROWS = HEAD_DIM + 2 * SUBLANES
HALO = SUBLANES


def _rms(xf, g):
    return xf * lax.rsqrt(jnp.mean(xf * xf, axis=-1, keepdims=True) + EPS) * g


def _in_proj_kernel(x_ref, g_ref, w_ref, cos_ref, sina_ref, sinb_ref, o_ref, u_ref):
    j = pl.program_id(1)

    @pl.when(j == 0)
    def _():
        u_ref[...] = _rms(x_ref[...], g_ref[...]).astype(jnp.bfloat16)

    def pieces(store):
        for c in range(D_MODEL // PROJ_SUB):
            w = w_ref[:, c * PROJ_SUB:(c + 1) * PROJ_SUB].astype(jnp.bfloat16)
            store(jnp.dot(u_ref[...], w, preferred_element_type=jnp.float32), c * PROJ_SUB)

    def rope_store(acc, col0):
        scale = jnp.where(j == 0, math.log2(math.e) / math.sqrt(HEAD_DIM), 1.0)
        cos, sina, sinb = cos_ref[...] * scale, sina_ref[...] * scale, sinb_ref[...] * scale
        for h in range(PROJ_SUB // HEAD_DIM):
            t = acc[:, h * HEAD_DIM:(h + 1) * HEAD_DIM]
            r = (t * cos + pltpu.roll(t, HEAD_DIM - ROPE_HALF, 1) * sina
                 + pltpu.roll(t, ROPE_HALF, 1) * sinb)
            o_ref[:, col0 + h * HEAD_DIM:col0 + (h + 1) * HEAD_DIM] = r.astype(o_ref.dtype)

    def plain_store(acc, col0):
        o_ref[:, col0:col0 + PROJ_SUB] = acc.astype(o_ref.dtype)

    @pl.when(j < 2)
    def _():
        pieces(rope_store)

    @pl.when(j >= 2)
    def _():
        pieces(plain_store)


def _in_proj(x2, g_mix, w_in, tables, seq):
    t = x2.shape[0]
    tiles_per_seq = seq // PROJ_TM

    def tab_spec(k):
        return pl.BlockSpec((None, PROJ_TM, HEAD_DIM), lambda i, j: (k, i % tiles_per_seq, 0))

    return pl.pallas_call(
        _in_proj_kernel,
        out_shape=jax.ShapeDtypeStruct((t, IN_WIDTH), jnp.bfloat16),
        grid=(t // PROJ_TM, N_SPLITS),
        in_specs=[
            pl.BlockSpec((PROJ_TM, D_MODEL), lambda i, j: (i, 0)),
            pl.BlockSpec((1, D_MODEL), lambda i, j: (0, 0)),
            pl.BlockSpec((D_MODEL, D_MODEL), lambda i, j: (0, j)),
            tab_spec(0), tab_spec(1), tab_spec(2),
        ],
        out_specs=pl.BlockSpec((PROJ_TM, D_MODEL), lambda i, j: (i, j)),
        scratch_shapes=[pltpu.VMEM((PROJ_TM, D_MODEL), jnp.bfloat16)],
        compiler_params=pltpu.CompilerParams(
            dimension_semantics=("arbitrary", "arbitrary"), vmem_limit_bytes=VMEM_LIMIT_BYTES),
        name="in_proj",
    )(x2, g_mix, w_in, tables, tables, tables)


def _reduce_rows(x, op, reduce_fn):
    rows = x.shape[0]
    while rows > SUBLANES and rows % (2 * SUBLANES) == 0:
        rows //= 2
        x = op(x[:rows], x[rows:])
    return reduce_fn(x, axis=0, keepdims=True)


def _moba_kernel(q_ref, k_ref, v_ref, o_ref, vt_ref, kmean_ref, sel_ref, qt_ref, stat_ref, *slot_refs, n_blocks):
    i = pl.program_id(2)
    f32, bf16 = jnp.float32, jnp.bfloat16
    heads = range(ATTN_HEADS)
    hsl = [slice(h * HEAD_DIM, (h + 1) * HEAD_DIM) for h in heads]
    nh = ATTN_HEADS
    ns = ATTN_SLOTS
    groups = [slot_refs[g * nh:(g + 1) * nh] for g in range(len(slot_refs) // nh)]
    s_ref = groups[0:ns]
    smax_ref = groups[ns:2 * ns]
    p_ref = groups[2 * ns:3 * ns]
    acc_ref = groups[3 * ns]

    @pl.when(i == 0)
    def _():
        row = lax.broadcasted_iota(jnp.int32, (VT_ROWS - HEAD_DIM, vt_ref.shape[2]), 0)
        for h in heads:
            vt_ref[h, HEAD_DIM:, :] = jnp.where(row == 0, 1.0, 0.0).astype(bf16)
            for c in range(n_blocks):
                rows = slice(c * MOBA_BLOCK, (c + 1) * MOBA_BLOCK)
                kmean_ref[h, c:c + 1, :] = jnp.mean(k_ref[rows, hsl[h]].astype(f32), axis=0, keepdims=True)
                vt_ref[h, :HEAD_DIM, rows] = v_ref[rows, hsl[h]].astype(f32).T.astype(bf16)

    def krows(blk):
        return pl.ds(pl.multiple_of(blk * MOBA_BLOCK, MOBA_BLOCK), MOBA_BLOCK)

    def qk(blk, slot):
        for h in heads:
            s = jnp.dot(k_ref[krows(blk), hsl[h]], qt_ref[h], preferred_element_type=f32)
            s_ref[slot][h][...] = s
            smax_ref[slot][h][...] = _reduce_rows(s, jnp.maximum, jnp.max)

    def pv(blk, slot):
        return [jnp.dot(vt_ref[h, :, krows(blk)], p_ref[slot][h][...], preferred_element_type=f32) for h in heads]

    def accumulate(pvs, alphas):
        for h in heads:
            acc_ref[h][...] = alphas[h] * acc_ref[h][...] + pvs[h]

    for h in heads:
        qt_ref[h] = q_ref[:, hsl[h]].astype(f32).T.astype(bf16)
    gates = [jnp.dot(kmean_ref[h].astype(bf16), qt_ref[h], preferred_element_type=f32) for h in heads]
    own_s = [jnp.dot(k_ref[krows(i), hsl[h]], qt_ref[h], preferred_element_type=f32) for h in heads]
    qk(0, 0)

    for h in heads:
        blk = lax.broadcasted_iota(jnp.int32, gates[h].shape, 0)
        past = blk < i
        g = jnp.where(past, gates[h], NEG)
        sel = jnp.zeros(g.shape, f32)
        for _ in range(MOBA_TOPK):
            mx = jnp.max(g, axis=0, keepdims=True)
            first = jnp.min(jnp.where(g == mx, blk, n_blocks), axis=0, keepdims=True)
            pick = blk == first
            sel = jnp.where(pick, 1.0, sel)
            g = jnp.where(pick, -jnp.inf, g)
        sel_ref[h] = jnp.where(past, sel, 0.0)

        key_idx = lax.broadcasted_iota(jnp.int32, own_s[h].shape, 0)
        qry_idx = lax.broadcasted_iota(jnp.int32, own_s[h].shape, 1)
        st = jnp.where(key_idx <= qry_idx, own_s[h], NEG)
        m0 = _reduce_rows(st, jnp.maximum, jnp.max)
        p_ref[ns - 2][h][...] = jnp.exp2((st - m0).astype(bf16))
        p_ref[ns - 1][h][...] = jnp.zeros(p_ref[ns - 1][h].shape, bf16)
        acc_ref[h][...] = jnp.zeros(acc_ref[h].shape, f32)
        stat_ref[h, 0], stat_ref[h, 1], stat_ref[h, 2] = m0, jnp.ones_like(m0), jnp.ones_like(m0)

    def pending_block(j):
        return jnp.where(j == 0, i, jnp.maximum(j - 2, 0))

    def stage(j, r):
        qk(jnp.minimum(j + 1, n_blocks - 1), (r + 1) % ns)
        pvs = pv(pending_block(j), (r - 2) % ns)
        old = [(stat_ref[h, 0], stat_ref[h, 1], stat_ref[h, 2]) for h in heads]
        for h in heads:
            m, a1, _ = old[h]
            on = sel_ref[h, pl.ds(j, 1), :] > 0.0
            m_new = jnp.maximum(m, jnp.where(on, smax_ref[r][h][...], NEG))
            shift = jnp.where(on, m_new, -NEG)
            p_ref[r][h][...] = jnp.exp2((s_ref[r][h][...] - shift).astype(bf16))
            stat_ref[h, 0], stat_ref[h, 1], stat_ref[h, 2] = m_new, jnp.exp2(m - m_new), a1
        accumulate(pvs, [st[2] for st in old])

    def trip(t, carry):
        for r in range(ns):
            stage(ns * t + r, r)
        return carry

    n_trips = i // ns
    done = n_trips * ns
    rem = i - done
    lax.fori_loop(0, n_trips, trip, 0)
    for half in range(ns // 2):
        @pl.when(rem > 2 * half)
        def _():
            stage(done + 2 * half, 2 * half)
            stage(done + 2 * half + 1, 2 * half + 1)
    n_done = done + 2 * ((rem + 1) // 2)

    def drain(slot):
        accumulate(pv(pending_block(n_done), slot), [stat_ref[h, 2] for h in heads])
        accumulate(pv(pending_block(n_done + 1), slot + 1), [stat_ref[h, 1] for h in heads])

    for half in range(ns // 2):
        @pl.when(n_done % ns == (2 * half + 2) % ns)
        def _():
            drain(2 * half)
    for h in heads:
        acc = acc_ref[h][...]
        out = acc[:HEAD_DIM] / acc[HEAD_DIM:HEAD_DIM + 1]
        o_ref[:, hsl[h]] = out.T.astype(o_ref.dtype)


def _moba_attention(proj3):
    b, s, _ = proj3.shape
    n_blocks = s // MOBA_BLOCK
    width = ATTN_HEADS * HEAD_DIM
    head_groups = N_HEADS // ATTN_HEADS
    return pl.pallas_call(
        functools.partial(_moba_kernel, n_blocks=n_blocks),
        out_shape=jax.ShapeDtypeStruct((b, s, N_HEADS * HEAD_DIM), jnp.bfloat16),
        grid=(b, head_groups, n_blocks),
        in_specs=[
            pl.BlockSpec((None, MOBA_BLOCK, width), lambda bi, h, i: (bi, i, h)),
            pl.BlockSpec((None, s, width), lambda bi, h, i: (bi, 0, head_groups + h)),
            pl.BlockSpec((None, s, width), lambda bi, h, i: (bi, 0, 2 * head_groups + h)),
        ],
        out_specs=pl.BlockSpec((None, MOBA_BLOCK, width), lambda bi, h, i: (bi, i, h)),
        scratch_shapes=[
            pltpu.VMEM((ATTN_HEADS, VT_ROWS, s), jnp.bfloat16),
            pltpu.VMEM((ATTN_HEADS, n_blocks, HEAD_DIM), jnp.float32),
            pltpu.VMEM((ATTN_HEADS, n_blocks, MOBA_BLOCK), jnp.float32),
            pltpu.VMEM((ATTN_HEADS, HEAD_DIM, MOBA_BLOCK), jnp.bfloat16),
            pltpu.VMEM((ATTN_HEADS, 3, 1, MOBA_BLOCK), jnp.float32),
        ] + [pltpu.VMEM((MOBA_BLOCK, MOBA_BLOCK), jnp.float32)] * (ATTN_SLOTS * ATTN_HEADS)
          + [pltpu.VMEM((1, MOBA_BLOCK), jnp.float32)] * (ATTN_SLOTS * ATTN_HEADS)
          + [pltpu.VMEM((MOBA_BLOCK, MOBA_BLOCK), jnp.bfloat16)] * (ATTN_SLOTS * ATTN_HEADS)
          + [pltpu.VMEM((VT_ROWS, MOBA_BLOCK), jnp.float32)] * ATTN_HEADS,
        compiler_params=pltpu.CompilerParams(
            dimension_semantics=("arbitrary", "arbitrary", "arbitrary"), vmem_limit_bytes=VMEM_LIMIT_BYTES),
        name="moba_attn",
    )(proj3, proj3, proj3)


def _mix_kernel(x_ref, attn_ref, cg_ref, bg_ref, xc_ref, ga_ref, gc_ref, cgh_ref, xch_ref,
                cw_ref, wab_ref, wcb_ref, wout_ref, o_ref, *, tiles_per_seq):
    f32 = jnp.float32
    i = pl.program_id(0)
    cx = cg_ref[...].astype(f32) * xc_ref[...].astype(f32)
    live = (i % tiles_per_seq != 0).astype(f32)
    halo = cgh_ref[...].astype(f32) * xch_ref[...].astype(f32) * live
    row = lax.broadcasted_iota(jnp.int32, cx.shape, 0)
    cw = cw_ref[...]
    z = cw[CONV_K - 1:CONV_K, :] * cx
    for d in range(1, CONV_K):
        shifted = pltpu.roll(cx, d, 0)
        for r in range(d):
            shifted = jnp.where(row == r, halo[HALO - d + r:HALO - d + r + 1, :], shifted)
        z = z + cw[CONV_K - 1 - d:CONV_K - d, :] * shifted
    yc = jnp.dot((bg_ref[...].astype(f32) * z).astype(jnp.bfloat16), wcb_ref[...], preferred_element_type=f32)
    ya = jnp.dot(attn_ref[...], wab_ref[...], preferred_element_type=f32)
    merged = jax.nn.sigmoid(ga_ref[...].astype(f32)) * ya + jax.nn.sigmoid(gc_ref[...].astype(f32)) * yc
    o_ref[...] = x_ref[...] + jnp.dot(merged.astype(jnp.bfloat16), wout_ref[...], preferred_element_type=f32)


def _mix(x2, attn2, proj, conv_w, wab, wcb, wout, seq):
    t = x2.shape[0]
    tiles_per_seq = seq // MIX_TM
    halo_per_tile = MIX_TM // HALO

    def col(c):
        return pl.BlockSpec((MIX_TM, D_MODEL), lambda i: (i, c))

    def halo(c):
        return pl.BlockSpec((HALO, D_MODEL), lambda i: (jnp.maximum(i * halo_per_tile - 1, 0), c))

    wspec = pl.BlockSpec((D_MODEL, D_MODEL), lambda i: (0, 0))
    return pl.pallas_call(
        functools.partial(_mix_kernel, tiles_per_seq=tiles_per_seq),
        out_shape=jax.ShapeDtypeStruct((t, D_MODEL), jnp.float32),
        grid=(t // MIX_TM,),
        in_specs=[
            pl.BlockSpec((MIX_TM, D_MODEL), lambda i: (i, 0)),
            pl.BlockSpec((MIX_TM, D_MODEL), lambda i: (i, 0)),
            col(3), col(4), col(5), col(6), col(7),
            halo(3), halo(5),
            pl.BlockSpec((CONV_K, D_MODEL), lambda i: (0, 0)),
            wspec, wspec, wspec,
        ],
        out_specs=pl.BlockSpec((MIX_TM, D_MODEL), lambda i: (i, 0)),
        compiler_params=pltpu.CompilerParams(
            dimension_semantics=("arbitrary",), vmem_limit_bytes=VMEM_LIMIT_BYTES),
        name="mix",
    )(x2, attn2, proj, proj, proj, proj, proj, proj, proj, conv_w, wab, wcb, wout)


def _ffn_kernel(h_ref, g_ref, wgu_ref, wd_ref, gf_ref, o_ref):
    f32 = jnp.float32
    h = h_ref[...]
    u = _rms(h, g_ref[...]).astype(jnp.bfloat16)
    ck = D_FF // FFN_CHUNKS
    acc = h
    for c in range(FFN_CHUNKS):
        gate = jnp.dot(u, wgu_ref[:, c * ck:(c + 1) * ck], preferred_element_type=f32)
        up = jnp.dot(u, wgu_ref[:, D_FF + c * ck:D_FF + (c + 1) * ck], preferred_element_type=f32)
        act = (gate * jax.nn.sigmoid(gate) * up).astype(jnp.bfloat16)
        acc = acc + jnp.dot(act, wd_ref[c * ck:(c + 1) * ck, :], preferred_element_type=f32)
    o_ref[...] = _rms(acc, gf_ref[...])


def _ffn(h1, g_ffn, wgu, wd, g_final):
    t = h1.shape[0]
    return pl.pallas_call(
        _ffn_kernel,
        out_shape=jax.ShapeDtypeStruct((t, D_MODEL), jnp.float32),
        grid=(t // FFN_TM,),
        in_specs=[
            pl.BlockSpec((FFN_TM, D_MODEL), lambda i: (i, 0)),
            pl.BlockSpec((1, D_MODEL), lambda i: (0, 0)),
            pl.BlockSpec((D_MODEL, 2 * D_FF), lambda i: (0, 0), pipeline_mode=pl.Buffered(1)),
            pl.BlockSpec((D_FF, D_MODEL), lambda i: (0, 0), pipeline_mode=pl.Buffered(1)),
            pl.BlockSpec((1, D_MODEL), lambda i: (0, 0)),
        ],
        out_specs=pl.BlockSpec((FFN_TM, D_MODEL), lambda i: (i, 0)),
        compiler_params=pltpu.CompilerParams(
            dimension_semantics=("arbitrary",), vmem_limit_bytes=VMEM_LIMIT_BYTES),
        name="ffn",
    )(h1, g_ffn, wgu, wd, g_final)


def _rope_tables(seq):
    inv_freq = ROPE_THETA ** (-jnp.arange(0, ROPE_DIM, 2, dtype=jnp.float32) / ROPE_DIM)
    freq = jnp.concatenate([inv_freq, inv_freq, jnp.zeros((HEAD_DIM - ROPE_DIM,), jnp.float32)])
    ang = jnp.arange(seq, dtype=jnp.int32).astype(jnp.float32)[:, None] * freq[None, :]
    sin = jnp.sin(ang)
    first = jnp.asarray(np.arange(HEAD_DIM) < ROPE_HALF)[None, :]
    return jnp.stack([jnp.cos(ang), jnp.where(first, -sin, 0.0), jnp.where(first, 0.0, sin)])


def kernel(x, g_mix, w_in, conv_w, w_attn_branch, w_conv_branch, w_out, g_ffn, w_gate_up, w_down, g_final):
    b, s, d = x.shape
    depth = w_in.shape[0]
    assert d == D_MODEL and s % PROJ_TM == 0 and s % MOBA_BLOCK == 0
    bf16 = jnp.bfloat16
    tables = _rope_tables(s)
    h = x.reshape(b * s, d)
    for l in range(depth):
        proj = _in_proj(h, g_mix[l][None, :], w_in[l], tables, s)
        attn = _moba_attention(proj.reshape(b, s, IN_WIDTH))
        h = _mix(h, attn.reshape(b * s, d), proj, conv_w[l], w_attn_branch[l].astype(bf16),
                 w_conv_branch[l].astype(bf16), w_out[l].astype(bf16), s)
        last = l == depth - 1
        assert last, "only DEPTH == 1 is supported"
        h = _ffn(h, g_ffn[l][None, :], w_gate_up[l].astype(bf16), w_down[l].astype(bf16), g_final[None, :])
    return h.reshape(b, s, d)
```

```python
import functools
import math

import jax
import jax.numpy as jnp
import numpy as np
from jax import lax
from jax.experimental import pallas as pl
from jax.experimental.pallas import tpu as pltpu

D_MODEL = 1024
N_HEADS = 8
HEAD_DIM = 128
MOBA_BLOCK = 256
MOBA_TOPK = 3
ROPE_THETA = 500000.0
ROPE_DIM = HEAD_DIM // 4
ROPE_HALF = ROPE_DIM // 2
CONV_K = 3
D_FF = 2816
EPS = 1e-6
NEG = -1e30
N_SPLITS = 8
IN_WIDTH = N_SPLITS * D_MODEL

VMEM_LIMIT_BYTES = 56 * 1024 * 1024
PROJ_TM = 2048
PROJ_SUB = 256
MIX_TM = 512
FFN_TM = 512
FFN_CHUNKS = 1
ATTN_HEADS = 4
ATTN_SLOTS = 4
ATTN_TRIP = 8
SUBLANES = 8
VT_ROWS = HEAD_DIM + 2 * SUBLANES
HALO = SUBLANES


def _rms(xf, g):
    return xf * lax.rsqrt(jnp.mean(xf * xf, axis=-1, keepdims=True) + EPS) * g


def _in_proj_kernel(x_ref, g_ref, w_ref, cos_ref, sina_ref, sinb_ref, o_ref, u_ref):
    j = pl.program_id(1)

    @pl.when(j == 0)
    def _():
        u_ref[...] = _rms(x_ref[...], g_ref[...]).astype(jnp.bfloat16)

    def pieces(store):
        for c in range(D_MODEL // PROJ_SUB):
            w = w_ref[:, c * PROJ_SUB:(c + 1) * PROJ_SUB].astype(jnp.bfloat16)
            store(jnp.dot(u_ref[...], w, preferred_element_type=jnp.float32), c * PROJ_SUB)

    def rope_store(acc, col0):
        scale = jnp.where(j == 0, math.log2(math.e) / math.sqrt(HEAD_DIM), 1.0)
        cos, sina, sinb = cos_ref[...] * scale, sina_ref[...] * scale, sinb_ref[...] * scale
        for h in range(PROJ_SUB // HEAD_DIM):
            t = acc[:, h * HEAD_DIM:(h + 1) * HEAD_DIM]
            r = (t * cos + pltpu.roll(t, HEAD_DIM - ROPE_HALF, 1) * sina
                 + pltpu.roll(t, ROPE_HALF, 1) * sinb)
            o_ref[:, col0 + h * HEAD_DIM:col0 + (h + 1) * HEAD_DIM] = r.astype(o_ref.dtype)

    def plain_store(acc, col0):
        o_ref[:, col0:col0 + PROJ_SUB] = acc.astype(o_ref.dtype)

    @pl.when(j < 2)
    def _():
        pieces(rope_store)

    @pl.when(j >= 2)
    def _():
        pieces(plain_store)


def _in_proj(x2, g_mix, w_in, tables, seq):
    t = x2.shape[0]
    tiles_per_seq = seq // PROJ_TM

    def tab_spec(k):
        return pl.BlockSpec((None, PROJ_TM, HEAD_DIM), lambda i, j: (k, i % tiles_per_seq, 0))

    return pl.pallas_call(
        _in_proj_kernel,
        out_shape=jax.ShapeDtypeStruct((t, IN_WIDTH), jnp.bfloat16),
        grid=(t // PROJ_TM, N_SPLITS),
        in_specs=[
            pl.BlockSpec((PROJ_TM, D_MODEL), lambda i, j: (i, 0)),
            pl.BlockSpec((1, D_MODEL), lambda i, j: (0, 0)),
            pl.BlockSpec((D_MODEL, D_MODEL), lambda i, j: (0, j)),
            tab_spec(0), tab_spec(1), tab_spec(2),
        ],
        out_specs=pl.BlockSpec((PROJ_TM, D_MODEL), lambda i, j: (i, j)),
        scratch_shapes=[pltpu.VMEM((PROJ_TM, D_MODEL), jnp.bfloat16)],
        compiler_params=pltpu.CompilerParams(
            dimension_semantics=("arbitrary", "arbitrary"), vmem_limit_bytes=VMEM_LIMIT_BYTES),
        name="in_proj",
    )(x2, g_mix, w_in, tables, tables, tables)


def _reduce_rows(x, op, reduce_fn):
    rows = x.shape[0]
    while rows > SUBLANES and rows % (2 * SUBLANES) == 0:
        rows //= 2
        x = op(x[:rows], x[rows:])
    return reduce_fn(x, axis=0, keepdims=True)


def _moba_kernel(q_ref, k_ref, v_ref, o_ref, vt_ref, kmean_ref, sel_ref, qt_ref, stat_ref, *slot_refs, n_blocks):
    i = pl.program_id(2)
    f32, bf16 = jnp.float32, jnp.bfloat16
    heads = range(ATTN_HEADS)
    hsl = [slice(h * HEAD_DIM, (h + 1) * HEAD_DIM) for h in heads]
    nh = ATTN_HEADS
    ns = ATTN_SLOTS
    groups = [slot_refs[g * nh:(g + 1) * nh] for g in range(len(slot_refs) // nh)]
    s_ref = groups[0:ns]
    smax_ref = groups[ns:2 * ns]
    p_ref = groups[2 * ns:3 * ns]
    acc_ref = groups[3 * ns]

    @pl.when(i == 0)
    def _():
        row = lax.broadcasted_iota(jnp.int32, (VT_ROWS - HEAD_DIM, vt_ref.shape[2]), 0)
        for h in heads:
            vt_ref[h, HEAD_DIM:, :] = jnp.where(row == 0, 1.0, 0.0).astype(bf16)
            for c in range(n_blocks):
                rows = slice(c * MOBA_BLOCK, (c + 1) * MOBA_BLOCK)
                kmean_ref[h, c:c + 1, :] = jnp.mean(k_ref[rows, hsl[h]].astype(f32), axis=0, keepdims=True)
                vt_ref[h, :HEAD_DIM, rows] = v_ref[rows, hsl[h]].astype(f32).T.astype(bf16)

    def krows(blk):
        return pl.ds(pl.multiple_of(blk * MOBA_BLOCK, MOBA_BLOCK), MOBA_BLOCK)

    def qk(blk, slot):
        for h in heads:
            s = jnp.dot(k_ref[krows(blk), hsl[h]], qt_ref[h], preferred_element_type=f32)
            s_ref[slot][h][...] = s
            smax_ref[slot][h][...] = _reduce_rows(s, jnp.maximum, jnp.max)

    def pv(blk, slot):
        return [jnp.dot(vt_ref[h, :, krows(blk)], p_ref[slot][h][...], preferred_element_type=f32) for h in heads]

    def accumulate(pvs, alphas):
        for h in heads:
            acc_ref[h][...] = alphas[h] * acc_ref[h][...] + pvs[h]

    for h in heads:
        qt_ref[h] = q_ref[:, hsl[h]].astype(f32).T.astype(bf16)
    gates = [jnp.dot(kmean_ref[h].astype(bf16), qt_ref[h], preferred_element_type=f32) for h in heads]
    own_s = [jnp.dot(k_ref[krows(i), hsl[h]], qt_ref[h], preferred_element_type=f32) for h in heads]
    qk(0, 0)

    for h in heads:
        blk = lax.broadcasted_iota(jnp.int32, gates[h].shape, 0)
        past = blk < i
        g = jnp.where(past, gates[h], NEG)
        sel = jnp.zeros(g.shape, f32)
        for _ in range(MOBA_TOPK):
            mx = jnp.max(g, axis=0, keepdims=True)
            first = jnp.min(jnp.where(g == mx, blk, n_blocks), axis=0, keepdims=True)
            pick = blk == first
            sel = jnp.where(pick, 1.0, sel)
            g = jnp.where(pick, -jnp.inf, g)
        sel_ref[h] = jnp.where(past, sel, 0.0)

        key_idx = lax.broadcasted_iota(jnp.int32, own_s[h].shape, 0)
        qry_idx = lax.broadcasted_iota(jnp.int32, own_s[h].shape, 1)
        st = jnp.where(key_idx <= qry_idx, own_s[h], NEG)
        m0 = _reduce_rows(st, jnp.maximum, jnp.max)
        p_ref[ns - 2][h][...] = jnp.exp2((st - m0).astype(bf16))
        p_ref[ns - 1][h][...] = jnp.zeros(p_ref[ns - 1][h].shape, bf16)
        acc_ref[h][...] = jnp.zeros(acc_ref[h].shape, f32)
        stat_ref[h, 0], stat_ref[h, 1], stat_ref[h, 2] = m0, jnp.ones_like(m0), jnp.ones_like(m0)

    def pending_block(j):
        return jnp.where(j == 0, i, jnp.maximum(j - 2, 0))

    def stage(j, r):
        qk(jnp.minimum(j + 1, n_blocks - 1), (r + 1) % ns)
        pvs = pv(pending_block(j), (r - 2) % ns)
        old = [(stat_ref[h, 0], stat_ref[h, 1], stat_ref[h, 2]) for h in heads]
        for h in heads:
            m, a1, _ = old[h]
            on = sel_ref[h, pl.ds(j, 1), :] > 0.0
            m_new = jnp.maximum(m, jnp.where(on, smax_ref[r][h][...], NEG))
            shift = jnp.where(on, m_new, -NEG)
            p_ref[r][h][...] = jnp.exp2((s_ref[r][h][...] - shift).astype(bf16))
            stat_ref[h, 0], stat_ref[h, 1], stat_ref[h, 2] = m_new, jnp.exp2(m - m_new), a1
        accumulate(pvs, [st[2] for st in old])

    def trip(t, carry):
        for r in range(ATTN_TRIP):
            stage(ATTN_TRIP * t + r, r % ns)
        return carry

    n_trips = i // ATTN_TRIP
    done = n_trips * ATTN_TRIP
    rem = i - done
    lax.fori_loop(0, n_trips, trip, 0)
    for pair in range(ATTN_TRIP // 2):
        @pl.when(rem > 2 * pair)
        def _():
            stage(done + 2 * pair, (2 * pair) % ns)
            stage(done + 2 * pair + 1, (2 * pair + 1) % ns)
    n_done = done + 2 * ((rem + 1) // 2)

    def drain(slot):
        pv2 = pv(pending_block(n_done), slot)
        pv1 = pv(pending_block(n_done + 1), slot + 1)
        for h in heads:
            a1, a2 = stat_ref[h, 1], stat_ref[h, 2]
            acc_ref[h][...] = a1 * (a2 * acc_ref[h][...] + pv2[h]) + pv1[h]

    for half in range(ns // 2):
        @pl.when(n_done % ns == (2 * half + 2) % ns)
        def _():
            drain(2 * half)
    for h in heads:
        acc = acc_ref[h][...]
        out = acc[:HEAD_DIM] / acc[HEAD_DIM:HEAD_DIM + 1]
        o_ref[:, hsl[h]] = out.T.astype(o_ref.dtype)


def _moba_attention(proj3):
    b, s, _ = proj3.shape
    n_blocks = s // MOBA_BLOCK
    width = ATTN_HEADS * HEAD_DIM
    head_groups = N_HEADS // ATTN_HEADS
    return pl.pallas_call(
        functools.partial(_moba_kernel, n_blocks=n_blocks),
        out_shape=jax.ShapeDtypeStruct((b, s, N_HEADS * HEAD_DIM), jnp.bfloat16),
        grid=(b, head_groups, n_blocks),
        in_specs=[
            pl.BlockSpec((None, MOBA_BLOCK, width), lambda bi, h, i: (bi, i, h)),
            pl.BlockSpec((None, s, width), lambda bi, h, i: (bi, 0, head_groups + h)),
            pl.BlockSpec((None, s, width), lambda bi, h, i: (bi, 0, 2 * head_groups + h)),
        ],
        out_specs=pl.BlockSpec((None, MOBA_BLOCK, width), lambda bi, h, i: (bi, i, h)),
        scratch_shapes=[
            pltpu.VMEM((ATTN_HEADS, VT_ROWS, s), jnp.bfloat16),
            pltpu.VMEM((ATTN_HEADS, n_blocks, HEAD_DIM), jnp.float32),
            pltpu.VMEM((ATTN_HEADS, n_blocks, MOBA_BLOCK), jnp.float32),
            pltpu.VMEM((ATTN_HEADS, HEAD_DIM, MOBA_BLOCK), jnp.bfloat16),
            pltpu.VMEM((ATTN_HEADS, 3, 1, MOBA_BLOCK), jnp.float32),
        ] + [pltpu.VMEM((MOBA_BLOCK, MOBA_BLOCK), jnp.float32)] * (ATTN_SLOTS * ATTN_HEADS)
          + [pltpu.VMEM((1, MOBA_BLOCK), jnp.float32)] * (ATTN_SLOTS * ATTN_HEADS)
          + [pltpu.VMEM((MOBA_BLOCK, MOBA_BLOCK), jnp.bfloat16)] * (ATTN_SLOTS * ATTN_HEADS)
          + [pltpu.VMEM((VT_ROWS, MOBA_BLOCK), jnp.float32)] * ATTN_HEADS,
        compiler_params=pltpu.CompilerParams(
            dimension_semantics=("arbitrary", "arbitrary", "arbitrary"), vmem_limit_bytes=VMEM_LIMIT_BYTES),
        name="moba_attn",
    )(proj3, proj3, proj3)


def _mix_kernel(x_ref, attn_ref, cg_ref, bg_ref, xc_ref, ga_ref, gc_ref, cgh_ref, xch_ref,
                cw_ref, wab_ref, wcb_ref, wout_ref, o_ref, *, tiles_per_seq):
    f32 = jnp.float32
    i = pl.program_id(0)
    cx = cg_ref[...].astype(f32) * xc_ref[...].astype(f32)
    live = (i % tiles_per_seq != 0).astype(f32)
    halo = cgh_ref[...].astype(f32) * xch_ref[...].astype(f32) * live
    row = lax.broadcasted_iota(jnp.int32, cx.shape, 0)
    cw = cw_ref[...]
    z = cw[CONV_K - 1:CONV_K, :] * cx
    for d in range(1, CONV_K):
        shifted = pltpu.roll(cx, d, 0)
        for r in range(d):
            shifted = jnp.where(row == r, halo[HALO - d + r:HALO - d + r + 1, :], shifted)
        z = z + cw[CONV_K - 1 - d:CONV_K - d, :] * shifted
    yc = jnp.dot((bg_ref[...].astype(f32) * z).astype(jnp.bfloat16), wcb_ref[...], preferred_element_type=f32)
    ya = jnp.dot(attn_ref[...], wab_ref[...], preferred_element_type=f32)
    merged = jax.nn.sigmoid(ga_ref[...].astype(f32)) * ya + jax.nn.sigmoid(gc_ref[...].astype(f32)) * yc
    o_ref[...] = x_ref[...] + jnp.dot(merged.astype(jnp.bfloat16), wout_ref[...], preferred_element_type=f32)


def _mix(x2, attn2, proj, conv_w, wab, wcb, wout, seq):
    t = x2.shape[0]
    tiles_per_seq = seq // MIX_TM
    halo_per_tile = MIX_TM // HALO

    def col(c):
        return pl.BlockSpec((MIX_TM, D_MODEL), lambda i: (i, c))

    def halo(c):
        return pl.BlockSpec((HALO, D_MODEL), lambda i: (jnp.maximum(i * halo_per_tile - 1, 0), c))

    wspec = pl.BlockSpec((D_MODEL, D_MODEL), lambda i: (0, 0))
    return pl.pallas_call(
        functools.partial(_mix_kernel, tiles_per_seq=tiles_per_seq),
        out_shape=jax.ShapeDtypeStruct((t, D_MODEL), jnp.float32),
        grid=(t // MIX_TM,),
        in_specs=[
            pl.BlockSpec((MIX_TM, D_MODEL), lambda i: (i, 0)),
            pl.BlockSpec((MIX_TM, D_MODEL), lambda i: (i, 0)),
            col(3), col(4), col(5), col(6), col(7),
            halo(3), halo(5),
            pl.BlockSpec((CONV_K, D_MODEL), lambda i: (0, 0)),
            wspec, wspec, wspec,
        ],
        out_specs=pl.BlockSpec((MIX_TM, D_MODEL), lambda i: (i, 0)),
        compiler_params=pltpu.CompilerParams(
            dimension_semantics=("arbitrary",), vmem_limit_bytes=VMEM_LIMIT_BYTES),
        name="mix",
    )(x2, attn2, proj, proj, proj, proj, proj, proj, proj, conv_w, wab, wcb, wout)


def _ffn_kernel(h_ref, g_ref, wgu_ref, wd_ref, gf_ref, o_ref):
    f32 = jnp.float32
    h = h_ref[...]
    u = _rms(h, g_ref[...]).astype(jnp.bfloat16)
    ck = D_FF // FFN_CHUNKS
    acc = h
    for c in range(FFN_CHUNKS):
        gate = jnp.dot(u, wgu_ref[:, c * ck:(c + 1) * ck], preferred_element_type=f32)
        up = jnp.dot(u, wgu_ref[:, D_FF + c * ck:D_FF + (c + 1) * ck], preferred_element_type=f32)
        act = (gate * jax.nn.sigmoid(gate) * up).astype(jnp.bfloat16)
        acc = acc + jnp.dot(act, wd_ref[c * ck:(c + 1) * ck, :], preferred_element_type=f32)
    o_ref[...] = _rms(acc, gf_ref[...])


def _ffn(h1, g_ffn, wgu, wd, g_final):
    t = h1.shape[0]
    return pl.pallas_call(
        _ffn_kernel,
        out_shape=jax.ShapeDtypeStruct((t, D_MODEL), jnp.float32),
        grid=(t // FFN_TM,),
        in_specs=[
            pl.BlockSpec((FFN_TM, D_MODEL), lambda i: (i, 0)),
            pl.BlockSpec((1, D_MODEL), lambda i: (0, 0)),
            pl.BlockSpec((D_MODEL, 2 * D_FF), lambda i: (0, 0), pipeline_mode=pl.Buffered(1)),
            pl.BlockSpec((D_FF, D_MODEL), lambda i: (0, 0), pipeline_mode=pl.Buffered(1)),
            pl.BlockSpec((1, D_MODEL), lambda i: (0, 0)),
        ],
        out_specs=pl.BlockSpec((FFN_TM, D_MODEL), lambda i: (i, 0)),
        compiler_params=pltpu.CompilerParams(
            dimension_semantics=("arbitrary",), vmem_limit_bytes=VMEM_LIMIT_BYTES),
        name="ffn",
    )(h1, g_ffn, wgu, wd, g_final)


def _rope_tables(seq):
    inv_freq = ROPE_THETA ** (-jnp.arange(0, ROPE_DIM, 2, dtype=jnp.float32) / ROPE_DIM)
    freq = jnp.concatenate([inv_freq, inv_freq, jnp.zeros((HEAD_DIM - ROPE_DIM,), jnp.float32)])
    ang = jnp.arange(seq, dtype=jnp.int32).astype(jnp.float32)[:, None] * freq[None, :]
    sin = jnp.sin(ang)
    first = jnp.asarray(np.arange(HEAD_DIM) < ROPE_HALF)[None, :]
    return jnp.stack([jnp.cos(ang), jnp.where(first, -sin, 0.0), jnp.where(first, 0.0, sin)])


def kernel(x, g_mix, w_in, conv_w, w_attn_branch, w_conv_branch, w_out, g_ffn, w_gate_up, w_down, g_final):
    b, s, d = x.shape
    depth = w_in.shape[0]
    assert d == D_MODEL and s % PROJ_TM == 0 and s % MOBA_BLOCK == 0
    bf16 = jnp.bfloat16
    tables = _rope_tables(s)
    h = x.reshape(b * s, d)
    for l in range(depth):
        proj = _in_proj(h, g_mix[l][None, :], w_in[l], tables, s)
        attn = _moba_attention(proj.reshape(b, s, IN_WIDTH))
        h = _mix(h, attn.reshape(b * s, d), proj, conv_w[l], w_attn_branch[l].astype(bf16),
                 w_conv_branch[l].astype(bf16), w_out[l].astype(bf16), s)
        last = l == depth - 1
        assert last, "only DEPTH == 1 is supported"
        h = _ffn(h, g_ffn[l][None, :], w_gate_up[l].astype(bf16), w_down[l].astype(bf16), g_final[None, :])
    return h.reshape(b, s, d)
```

```python
import functools
import math

import jax
import jax.numpy as jnp
import numpy as np
from jax import lax
from jax.experimental import pallas as pl
from jax.experimental.pallas import tpu as pltpu

D_MODEL = 1024
N_HEADS = 8
HEAD_DIM = 128
MOBA_BLOCK = 256
MOBA_TOPK = 3
ROPE_THETA = 500000.0
ROPE_DIM = HEAD_DIM // 4
ROPE_HALF = ROPE_DIM // 2
CONV_K = 3
D_FF = 2816
EPS = 1e-6
NEG = -1e30
N_SPLITS = 8
IN_WIDTH = N_SPLITS * D_MODEL

VMEM_LIMIT_BYTES = 56 * 1024 * 1024
PROJ_TM = 2048
PROJ_SUB = 256
MIX_TM = 512
FFN_TM = 512
FFN_CHUNKS = 1
ATTN_HEADS = 4
ATTN_PAIR = 2
ATTN_SLOTS = 4
SUBLANES = 8
VT_ROWS = HEAD_DIM + 2 * SUBLANES
HALO = SUBLANES


def _rms(xf, g):
    return xf * lax.rsqrt(jnp.mean(xf * xf, axis=-1, keepdims=True) + EPS) * g


def _in_proj_kernel(x_ref, g_ref, w_ref, cos_ref, sina_ref, sinb_ref, o_ref, u_ref):
    j = pl.program_id(1)

    @pl.when(j == 0)
    def _():
        u_ref[...] = _rms(x_ref[...], g_ref[...]).astype(jnp.bfloat16)

    def pieces(store):
        for c in range(D_MODEL // PROJ_SUB):
            w = w_ref[:, c * PROJ_SUB:(c + 1) * PROJ_SUB].astype(jnp.bfloat16)
            store(jnp.dot(u_ref[...], w, preferred_element_type=jnp.float32), c * PROJ_SUB)

    def rope_store(acc, col0):
        scale = jnp.where(j == 0, math.log2(math.e) / math.sqrt(HEAD_DIM), 1.0)
        cos, sina, sinb = cos_ref[...] * scale, sina_ref[...] * scale, sinb_ref[...] * scale
        for h in range(PROJ_SUB // HEAD_DIM):
            t = acc[:, h * HEAD_DIM:(h + 1) * HEAD_DIM]
            r = (t * cos + pltpu.roll(t, HEAD_DIM - ROPE_HALF, 1) * sina
                 + pltpu.roll(t, ROPE_HALF, 1) * sinb)
            o_ref[:, col0 + h * HEAD_DIM:col0 + (h + 1) * HEAD_DIM] = r.astype(o_ref.dtype)

    def plain_store(acc, col0):
        o_ref[:, col0:col0 + PROJ_SUB] = acc.astype(o_ref.dtype)

    @pl.when(j < 2)
    def _():
        pieces(rope_store)

    @pl.when(j >= 2)
    def _():
        pieces(plain_store)


def _in_proj(x2, g_mix, w_in, tables, seq):
    t = x2.shape[0]
    tiles_per_seq = seq // PROJ_TM

    def tab_spec(k):
        return pl.BlockSpec((None, PROJ_TM, HEAD_DIM), lambda i, j: (k, i % tiles_per_seq, 0))

    return pl.pallas_call(
        _in_proj_kernel,
        out_shape=jax.ShapeDtypeStruct((t, IN_WIDTH), jnp.bfloat16),
        grid=(t // PROJ_TM, N_SPLITS),
        in_specs=[
            pl.BlockSpec((PROJ_TM, D_MODEL), lambda i, j: (i, 0)),
            pl.BlockSpec((1, D_MODEL), lambda i, j: (0, 0)),
            pl.BlockSpec((D_MODEL, D_MODEL), lambda i, j: (0, j)),
            tab_spec(0), tab_spec(1), tab_spec(2),
        ],
        out_specs=pl.BlockSpec((PROJ_TM, D_MODEL), lambda i, j: (i, j)),
        scratch_shapes=[pltpu.VMEM((PROJ_TM, D_MODEL), jnp.bfloat16)],
        compiler_params=pltpu.CompilerParams(
            dimension_semantics=("arbitrary", "arbitrary"), vmem_limit_bytes=VMEM_LIMIT_BYTES),
        name="in_proj",
    )(x2, g_mix, w_in, tables, tables, tables)


def _reduce_rows(x, op, reduce_fn):
    rows = x.shape[0]
    while rows > SUBLANES and rows % (2 * SUBLANES) == 0:
        rows //= 2
        x = op(x[:rows], x[rows:])
    return reduce_fn(x, axis=0, keepdims=True)


def _moba_kernel(q_ref, k_ref, v_ref, o_ref, vt_ref, kmean_ref, sel_ref, qt_ref, stat_ref, *slot_refs, n_blocks):
    i = pl.program_id(2)
    f32, bf16 = jnp.float32, jnp.bfloat16
    heads = range(ATTN_HEADS)
    hsl = [slice(h * HEAD_DIM, (h + 1) * HEAD_DIM) for h in heads]
    nh, ns, gk = ATTN_HEADS, ATTN_SLOTS, ATTN_PAIR * MOBA_BLOCK
    n_pairs_total = n_blocks // ATTN_PAIR
    groups = [slot_refs[g * nh:(g + 1) * nh] for g in range(len(slot_refs) // nh)]
    s_ref = groups[0:2]
    smax_ref = groups[2:4]
    p_ref = groups[4:4 + ns]
    acc_ref = groups[4 + ns]

    @pl.when(i == 0)
    def _():
        row = lax.broadcasted_iota(jnp.int32, (VT_ROWS - HEAD_DIM, vt_ref.shape[2]), 0)
        for h in heads:
            vt_ref[h, HEAD_DIM:, :] = jnp.where(row == 0, 1.0, 0.0).astype(bf16)
            for c in range(n_blocks):
                rows = slice(c * MOBA_BLOCK, (c + 1) * MOBA_BLOCK)
                kmean_ref[h, c:c + 1, :] = jnp.mean(k_ref[rows, hsl[h]].astype(f32), axis=0, keepdims=True)
                vt_ref[h, :HEAD_DIM, rows] = v_ref[rows, hsl[h]].astype(f32).T.astype(bf16)

    def krows(idx, size):
        return pl.ds(pl.multiple_of(idx * size, size), size)

    def qk(pair, slot):
        for h in heads:
            s = jnp.dot(k_ref[krows(pair, gk), hsl[h]], qt_ref[h], preferred_element_type=f32)
            s_ref[slot][h][...] = s
            for b in range(ATTN_PAIR):
                blk_s = s[b * MOBA_BLOCK:(b + 1) * MOBA_BLOCK]
                smax_ref[slot][h][b:b + 1, :] = _reduce_rows(blk_s, jnp.maximum, jnp.max)

    def pv(pair, slot):
        return [jnp.dot(vt_ref[h, :, krows(pair, gk)], p_ref[slot][h][...], preferred_element_type=f32)
                for h in heads]

    for h in heads:
        qt_ref[h] = q_ref[:, hsl[h]].astype(f32).T.astype(bf16)
    gates = [jnp.dot(kmean_ref[h].astype(bf16), qt_ref[h], preferred_element_type=f32) for h in heads]
    own_s = [jnp.dot(k_ref[krows(i, MOBA_BLOCK), hsl[h]], qt_ref[h], preferred_element_type=f32) for h in heads]
    qk(0, 0)

    for h in heads:
        blk = lax.broadcasted_iota(jnp.int32, gates[h].shape, 0)
        past = blk < i
        g = jnp.where(past, gates[h], NEG)
        sel = jnp.zeros(g.shape, f32)
        for _ in range(MOBA_TOPK):
            mx = jnp.max(g, axis=0, keepdims=True)
            first = jnp.min(jnp.where(g == mx, blk, n_blocks), axis=0, keepdims=True)
            pick = blk == first
            sel = jnp.where(pick, 1.0, sel)
            g = jnp.where(pick, -jnp.inf, g)
        sel_ref[h] = jnp.where(past, sel, 0.0)

        key_idx = lax.broadcasted_iota(jnp.int32, own_s[h].shape, 0)
        qry_idx = lax.broadcasted_iota(jnp.int32, own_s[h].shape, 1)
        st = jnp.where(key_idx <= qry_idx, own_s[h], NEG)
        m0 = _reduce_rows(st, jnp.maximum, jnp.max)
        p_ref[ns - 2][h][...] = jnp.zeros(p_ref[ns - 2][h].shape, bf16)
        p_ref[ns - 2][h][krows(i % ATTN_PAIR, MOBA_BLOCK), :] = jnp.exp2((st - m0).astype(bf16))
        p_ref[ns - 1][h][...] = jnp.zeros(p_ref[ns - 1][h].shape, bf16)
        acc_ref[h][...] = jnp.zeros(acc_ref[h].shape, f32)
        stat_ref[h, 0], stat_ref[h, 1], stat_ref[h, 2] = m0, jnp.ones_like(m0), jnp.ones_like(m0)

    def pending_pair(j):
        return jnp.where(j == 0, i // ATTN_PAIR, jnp.maximum(j - 2, 0))

    def stage(j, r):
        qk(jnp.minimum(j + 1, n_pairs_total - 1), (r + 1) % 2)
        pvs = pv(pending_pair(j), (r - 2) % ns)
        old = [(stat_ref[h, 0], stat_ref[h, 1], stat_ref[h, 2]) for h in heads]
        for h in heads:
            m, a1, a2 = old[h]
            on = [sel_ref[h, pl.ds(j * ATTN_PAIR + b, 1), :] > 0.0 for b in range(ATTN_PAIR)]
            m_new = m
            for b in range(ATTN_PAIR):
                m_new = jnp.maximum(m_new, jnp.where(on[b], smax_ref[r % 2][h][b:b + 1, :], NEG))
            for b in range(ATTN_PAIR):
                rows = slice(b * MOBA_BLOCK, (b + 1) * MOBA_BLOCK)
                shift = jnp.where(on[b], m_new, -NEG)
                p_ref[r][h][rows, :] = jnp.exp2((s_ref[r % 2][h][rows, :] - shift).astype(bf16))
            stat_ref[h, 0], stat_ref[h, 1], stat_ref[h, 2] = m_new, jnp.exp2(m - m_new), a1
            acc_ref[h][...] = a2 * acc_ref[h][...] + pvs[h]

    def trip(t, carry):
        for r in range(ns):
            stage(ns * t + r, r)
        return carry

    n_pairs = (i + ATTN_PAIR - 1) // ATTN_PAIR
    n_trips = n_pairs // ns
    done = n_trips * ns
    lax.fori_loop(0, n_trips, trip, 0)
    for r in range(ns - 1):
        @pl.when(n_pairs - done > r)
        def _():
            stage(done + r, r)

    for r in range(ns):
        @pl.when(n_pairs % ns == r)
        def _():
            pv2 = pv(pending_pair(n_pairs), (r - 2) % ns)
            pv1 = pv(pending_pair(n_pairs + 1), (r - 1) % ns)
            for h in heads:
                a1, a2 = stat_ref[h, 1], stat_ref[h, 2]
                acc = a1 * (a2 * acc_ref[h][...] + pv2[h]) + pv1[h]
                out = acc[:HEAD_DIM] / acc[HEAD_DIM:HEAD_DIM + 1]
                o_ref[:, hsl[h]] = out.T.astype(o_ref.dtype)


def _moba_attention(proj3):
    b, s, _ = proj3.shape
    n_blocks = s // MOBA_BLOCK
    width = ATTN_HEADS * HEAD_DIM
    head_groups = N_HEADS // ATTN_HEADS
    return pl.pallas_call(
        functools.partial(_moba_kernel, n_blocks=n_blocks),
        out_shape=jax.ShapeDtypeStruct((b, s, N_HEADS * HEAD_DIM), jnp.bfloat16),
        grid=(b, head_groups, n_blocks),
        in_specs=[
            pl.BlockSpec((None, MOBA_BLOCK, width), lambda bi, h, i: (bi, i, h)),
            pl.BlockSpec((None, s, width), lambda bi, h, i: (bi, 0, head_groups + h)),
            pl.BlockSpec((None, s, width), lambda bi, h, i: (bi, 0, 2 * head_groups + h)),
        ],
        out_specs=pl.BlockSpec((None, MOBA_BLOCK, width), lambda bi, h, i: (bi, i, h)),
        scratch_shapes=[
            pltpu.VMEM((ATTN_HEADS, VT_ROWS, s), jnp.bfloat16),
            pltpu.VMEM((ATTN_HEADS, n_blocks, HEAD_DIM), jnp.float32),
            pltpu.VMEM((ATTN_HEADS, n_blocks, MOBA_BLOCK), jnp.float32),
            pltpu.VMEM((ATTN_HEADS, HEAD_DIM, MOBA_BLOCK), jnp.bfloat16),
            pltpu.VMEM((ATTN_HEADS, 3, 1, MOBA_BLOCK), jnp.float32),
        ] + [pltpu.VMEM((ATTN_PAIR * MOBA_BLOCK, MOBA_BLOCK), jnp.float32)] * (2 * ATTN_HEADS)
          + [pltpu.VMEM((ATTN_PAIR, MOBA_BLOCK), jnp.float32)] * (2 * ATTN_HEADS)
          + [pltpu.VMEM((ATTN_PAIR * MOBA_BLOCK, MOBA_BLOCK), jnp.bfloat16)] * (ATTN_SLOTS * ATTN_HEADS)
          + [pltpu.VMEM((VT_ROWS, MOBA_BLOCK), jnp.float32)] * ATTN_HEADS,
        compiler_params=pltpu.CompilerParams(
            dimension_semantics=("arbitrary", "arbitrary", "arbitrary"), vmem_limit_bytes=VMEM_LIMIT_BYTES),
        name="moba_attn",
    )(proj3, proj3, proj3)


def _mix_kernel(x_ref, attn_ref, cg_ref, bg_ref, xc_ref, ga_ref, gc_ref, cgh_ref, xch_ref,
                cw_ref, wab_ref, wcb_ref, wout_ref, o_ref, *, tiles_per_seq):
    f32 = jnp.float32
    i = pl.program_id(0)
    cx = cg_ref[...].astype(f32) * xc_ref[...].astype(f32)
    live = (i % tiles_per_seq != 0).astype(f32)
    halo = cgh_ref[...].astype(f32) * xch_ref[...].astype(f32) * live
    row = lax.broadcasted_iota(jnp.int32, cx.shape, 0)
    cw = cw_ref[...]
    z = cw[CONV_K - 1:CONV_K, :] * cx
    for d in range(1, CONV_K):
        shifted = pltpu.roll(cx, d, 0)
        for r in range(d):
            shifted = jnp.where(row == r, halo[HALO - d + r:HALO - d + r + 1, :], shifted)
        z = z + cw[CONV_K - 1 - d:CONV_K - d, :] * shifted
    yc = jnp.dot((bg_ref[...].astype(f32) * z).astype(jnp.bfloat16), wcb_ref[...], preferred_element_type=f32)
    ya = jnp.dot(attn_ref[...], wab_ref[...], preferred_element_type=f32)
    merged = jax.nn.sigmoid(ga_ref[...].astype(f32)) * ya + jax.nn.sigmoid(gc_ref[...].astype(f32)) * yc
    o_ref[...] = x_ref[...] + jnp.dot(merged.astype(jnp.bfloat16), wout_ref[...], preferred_element_type=f32)


def _mix(x2, attn2, proj, conv_w, wab, wcb, wout, seq):
    t = x2.shape[0]
    tiles_per_seq = seq // MIX_TM
    halo_per_tile = MIX_TM // HALO

    def col(c):
        return pl.BlockSpec((MIX_TM, D_MODEL), lambda i: (i, c))

    def halo(c):
        return pl.BlockSpec((HALO, D_MODEL), lambda i: (jnp.maximum(i * halo_per_tile - 1, 0), c))

    wspec = pl.BlockSpec((D_MODEL, D_MODEL), lambda i: (0, 0))
    return pl.pallas_call(
        functools.partial(_mix_kernel, tiles_per_seq=tiles_per_seq),
        out_shape=jax.ShapeDtypeStruct((t, D_MODEL), jnp.float32),
        grid=(t // MIX_TM,),
        in_specs=[
            pl.BlockSpec((MIX_TM, D_MODEL), lambda i: (i, 0)),
            pl.BlockSpec((MIX_TM, D_MODEL), lambda i: (i, 0)),
            col(3), col(4), col(5), col(6), col(7),
            halo(3), halo(5),
            pl.BlockSpec((CONV_K, D_MODEL), lambda i: (0, 0)),
            wspec, wspec, wspec,
        ],
        out_specs=pl.BlockSpec((MIX_TM, D_MODEL), lambda i: (i, 0)),
        compiler_params=pltpu.CompilerParams(
            dimension_semantics=("arbitrary",), vmem_limit_bytes=VMEM_LIMIT_BYTES),
        name="mix",
    )(x2, attn2, proj, proj, proj, proj, proj, proj, proj, conv_w, wab, wcb, wout)


def _ffn_kernel(h_ref, g_ref, wgu_ref, wd_ref, gf_ref, o_ref):
    f32 = jnp.float32
    h = h_ref[...]
    u = _rms(h, g_ref[...]).astype(jnp.bfloat16)
    ck = D_FF // FFN_CHUNKS
    acc = h
    for c in range(FFN_CHUNKS):
        gate = jnp.dot(u, wgu_ref[:, c * ck:(c + 1) * ck], preferred_element_type=f32)
        up = jnp.dot(u, wgu_ref[:, D_FF + c * ck:D_FF + (c + 1) * ck], preferred_element_type=f32)
        act = (gate * jax.nn.sigmoid(gate) * up).astype(jnp.bfloat16)
        acc = acc + jnp.dot(act, wd_ref[c * ck:(c + 1) * ck, :], preferred_element_type=f32)
    o_ref[...] = _rms(acc, gf_ref[...])


def _ffn(h1, g_ffn, wgu, wd, g_final):
    t = h1.shape[0]
    return pl.pallas_call(
        _ffn_kernel,
        out_shape=jax.ShapeDtypeStruct((t, D_MODEL), jnp.float32),
        grid=(t // FFN_TM,),
        in_specs=[
            pl.BlockSpec((FFN_TM, D_MODEL), lambda i: (i, 0)),
            pl.BlockSpec((1, D_MODEL), lambda i: (0, 0)),
            pl.BlockSpec((D_MODEL, 2 * D_FF), lambda i: (0, 0), pipeline_mode=pl.Buffered(1)),
            pl.BlockSpec((D_FF, D_MODEL), lambda i: (0, 0), pipeline_mode=pl.Buffered(1)),
            pl.BlockSpec((1, D_MODEL), lambda i: (0, 0)),
        ],
        out_specs=pl.BlockSpec((FFN_TM, D_MODEL), lambda i: (i, 0)),
        compiler_params=pltpu.CompilerParams(
            dimension_semantics=("arbitrary",), vmem_limit_bytes=VMEM_LIMIT_BYTES),
        name="ffn",
    )(h1, g_ffn, wgu, wd, g_final)


def _rope_tables(seq):
    inv_freq = ROPE_THETA ** (-jnp.arange(0, ROPE_DIM, 2, dtype=jnp.float32) / ROPE_DIM)
    freq = jnp.concatenate([inv_freq, inv_freq, jnp.zeros((HEAD_DIM - ROPE_DIM,), jnp.float32)])
    ang = jnp.arange(seq, dtype=jnp.int32).astype(jnp.float32)[:, None] * freq[None, :]
    sin = jnp.sin(ang)
    first = jnp.asarray(np.arange(HEAD_DIM) < ROPE_HALF)[None, :]
    return jnp.stack([jnp.cos(ang), jnp.where(first, -sin, 0.0), jnp.where(first, 0.0, sin)])


def kernel(x, g_mix, w_in, conv_w, w_attn_branch, w_conv_branch, w_out, g_ffn, w_gate_up, w_down, g_final):
    b, s, d = x.shape
    depth = w_in.shape[0]
    assert d == D_MODEL and s % PROJ_TM == 0 and s % (ATTN_PAIR * MOBA_BLOCK) == 0
    bf16 = jnp.bfloat16
    tables = _rope_tables(s)
    h = x.reshape(b * s, d)
    for l in range(depth):
        proj = _in_proj(h, g_mix[l][None, :], w_in[l], tables, s)
        attn = _moba_attention(proj.reshape(b, s, IN_WIDTH))
        h = _mix(h, attn.reshape(b * s, d), proj, conv_w[l], w_attn_branch[l].astype(bf16),
                 w_conv_branch[l].astype(bf16), w_out[l].astype(bf16), s)
        last = l == depth - 1
        assert last, "only DEPTH == 1 is supported"
        h = _ffn(h, g_ffn[l][None, :], w_gate_up[l].astype(bf16), w_down[l].astype(bf16), g_final[None, :])
    return h.reshape(b, s, d)
```

```python
import functools
import math

import jax
import jax.numpy as jnp
import numpy as np
from jax import lax
from jax.experimental import pallas as pl
from jax.experimental.pallas import tpu as pltpu

D_MODEL = 1024
N_HEADS = 8
HEAD_DIM = 128
MOBA_BLOCK = 256
MOBA_TOPK = 3
ROPE_THETA = 500000.0
ROPE_DIM = HEAD_DIM // 4
ROPE_HALF = ROPE_DIM // 2
CONV_K = 3
D_FF = 2816
EPS = 1e-6
NEG = -1e30
N_SPLITS = 8
IN_WIDTH = N_SPLITS * D_MODEL

VMEM_LIMIT_BYTES = 56 * 1024 * 1024
PROJ_TM = 2048
PROJ_SUB = 256
MIX_TM = 512
FFN_TM = 512
FFN_CHUNKS = 1
ATTN_HEADS = 4
ATTN_PAIR = 2
ATTN_SLOTS = 4
SUBLANES = 8
VT_ROWS = HEAD_DIM + 2 * SUBLANES
HALO = SUBLANES


def _rms(xf, g):
    return xf * lax.rsqrt(jnp.mean(xf * xf, axis=-1, keepdims=True) + EPS) * g


def _in_proj_kernel(x_ref, g_ref, w_ref, cos_ref, sina_ref, sinb_ref, o_ref, u_ref):
    j = pl.program_id(1)

    @pl.when(j == 0)
    def _():
        u_ref[...] = _rms(x_ref[...], g_ref[...]).astype(jnp.bfloat16)

    def pieces(store):
        for c in range(D_MODEL // PROJ_SUB):
            w = w_ref[:, c * PROJ_SUB:(c + 1) * PROJ_SUB].astype(jnp.bfloat16)
            store(jnp.dot(u_ref[...], w, preferred_element_type=jnp.float32), c * PROJ_SUB)

    def rope_store(acc, col0):
        scale = jnp.where(j == 0, math.log2(math.e) / math.sqrt(HEAD_DIM), 1.0)
        cos, sina, sinb = cos_ref[...] * scale, sina_ref[...] * scale, sinb_ref[...] * scale
        for h in range(PROJ_SUB // HEAD_DIM):
            t = acc[:, h * HEAD_DIM:(h + 1) * HEAD_DIM]
            r = (t * cos + pltpu.roll(t, HEAD_DIM - ROPE_HALF, 1) * sina
                 + pltpu.roll(t, ROPE_HALF, 1) * sinb)
            o_ref[:, col0 + h * HEAD_DIM:col0 + (h + 1) * HEAD_DIM] = r.astype(o_ref.dtype)

    def plain_store(acc, col0):
        o_ref[:, col0:col0 + PROJ_SUB] = acc.astype(o_ref.dtype)

    @pl.when(j < 2)
    def _():
        pieces(rope_store)

    @pl.when(j >= 2)
    def _():
        pieces(plain_store)


def _in_proj(x2, g_mix, w_in, tables, seq):
    t = x2.shape[0]
    tiles_per_seq = seq // PROJ_TM

    def tab_spec(k):
        return pl.BlockSpec((None, PROJ_TM, HEAD_DIM), lambda i, j: (k, i % tiles_per_seq, 0))

    return pl.pallas_call(
        _in_proj_kernel,
        out_shape=jax.ShapeDtypeStruct((t, IN_WIDTH), jnp.bfloat16),
        grid=(t // PROJ_TM, N_SPLITS),
        in_specs=[
            pl.BlockSpec((PROJ_TM, D_MODEL), lambda i, j: (i, 0)),
            pl.BlockSpec((1, D_MODEL), lambda i, j: (0, 0)),
            pl.BlockSpec((D_MODEL, D_MODEL), lambda i, j: (0, j)),
            tab_spec(0), tab_spec(1), tab_spec(2),
        ],
        out_specs=pl.BlockSpec((PROJ_TM, D_MODEL), lambda i, j: (i, j)),
        scratch_shapes=[pltpu.VMEM((PROJ_TM, D_MODEL), jnp.bfloat16)],
        compiler_params=pltpu.CompilerParams(
            dimension_semantics=("arbitrary", "arbitrary"), vmem_limit_bytes=VMEM_LIMIT_BYTES),
        name="in_proj",
    )(x2, g_mix, w_in, tables, tables, tables)


def _reduce_rows(x, op, reduce_fn):
    rows = x.shape[0]
    while rows > SUBLANES and rows % (2 * SUBLANES) == 0:
        rows //= 2
        x = op(x[:rows], x[rows:])
    return reduce_fn(x, axis=0, keepdims=True)


def _moba_kernel(q_ref, k_ref, v_ref, o_ref, vt_ref, kmean_ref, sel_ref, qt_ref, stat_ref, *slot_refs, n_blocks):
    i = pl.program_id(2)
    f32, bf16 = jnp.float32, jnp.bfloat16
    heads = range(ATTN_HEADS)
    hsl = [slice(h * HEAD_DIM, (h + 1) * HEAD_DIM) for h in heads]
    nh, ns, gk = ATTN_HEADS, ATTN_SLOTS, ATTN_PAIR * MOBA_BLOCK
    n_pairs_total = n_blocks // ATTN_PAIR
    groups = [slot_refs[g * nh:(g + 1) * nh] for g in range(len(slot_refs) // nh)]
    s_ref = groups[0:2]
    smax_ref = groups[2:4]
    p_ref = groups[4:4 + ns]
    acc_ref = groups[4 + ns]

    @pl.when(i == 0)
    def _():
        row = lax.broadcasted_iota(jnp.int32, (VT_ROWS - HEAD_DIM, vt_ref.shape[2]), 0)
        for h in heads:
            vt_ref[h, HEAD_DIM:, :] = jnp.where(row == 0, 1.0, 0.0).astype(bf16)
            for c in range(n_blocks):
                rows = slice(c * MOBA_BLOCK, (c + 1) * MOBA_BLOCK)
                kmean_ref[h, c:c + 1, :] = jnp.mean(k_ref[rows, hsl[h]].astype(f32), axis=0, keepdims=True)
                vt_ref[h, :HEAD_DIM, rows] = v_ref[rows, hsl[h]].astype(f32).T.astype(bf16)

    def krows(idx, size):
        return pl.ds(pl.multiple_of(idx * size, size), size)

    def qk(pair, slot):
        for h in heads:
            s = jnp.dot(k_ref[krows(pair, gk), hsl[h]], qt_ref[h], preferred_element_type=f32)
            s_ref[slot][h][...] = s
            for b in range(ATTN_PAIR):
                blk_s = s[b * MOBA_BLOCK:(b + 1) * MOBA_BLOCK]
                smax_ref[slot][h][b:b + 1, :] = _reduce_rows(blk_s, jnp.maximum, jnp.max)

    def pv(pair, slot):
        return [jnp.dot(vt_ref[h, :, krows(pair, gk)], p_ref[slot][h][...], preferred_element_type=f32)
                for h in heads]

    for h in heads:
        qt_ref[h] = q_ref[:, hsl[h]].astype(f32).T.astype(bf16)
    gates = [jnp.dot(kmean_ref[h].astype(bf16), qt_ref[h], preferred_element_type=f32) for h in heads]
    own_s = [jnp.dot(k_ref[krows(i, MOBA_BLOCK), hsl[h]], qt_ref[h], preferred_element_type=f32) for h in heads]
    qk(0, 0)

    for h in heads:
        blk = lax.broadcasted_iota(jnp.int32, gates[h].shape, 0)
        past = blk < i
        g = jnp.where(past, gates[h], NEG)
        sel = jnp.zeros(g.shape, f32)
        for _ in range(MOBA_TOPK):
            mx = jnp.max(g, axis=0, keepdims=True)
            first = jnp.min(jnp.where(g == mx, blk, n_blocks), axis=0, keepdims=True)
            pick = blk == first
            sel = jnp.where(pick, 1.0, sel)
            g = jnp.where(pick, -jnp.inf, g)
        sel_ref[h] = jnp.where(past, sel, 0.0)

        key_idx = lax.broadcasted_iota(jnp.int32, own_s[h].shape, 0)
        qry_idx = lax.broadcasted_iota(jnp.int32, own_s[h].shape, 1)
        st = jnp.where(key_idx <= qry_idx, own_s[h], NEG)
        m0 = _reduce_rows(st, jnp.maximum, jnp.max)
        p_ref[ns - 2][h][...] = jnp.zeros(p_ref[ns - 2][h].shape, bf16)
        p_ref[ns - 2][h][krows(i % ATTN_PAIR, MOBA_BLOCK), :] = jnp.exp2((st - m0).astype(bf16))
        p_ref[ns - 1][h][...] = jnp.zeros(p_ref[ns - 1][h].shape, bf16)
        acc_ref[h][...] = jnp.zeros(acc_ref[h].shape, f32)
        stat_ref[h, 0], stat_ref[h, 1], stat_ref[h, 2] = m0, jnp.ones_like(m0), jnp.ones_like(m0)

    def pending_pair(j):
        return jnp.where(j == 0, i // ATTN_PAIR, jnp.maximum(j - 2, 0))

    def stage(j, r):
        qk(jnp.minimum(j + 1, n_pairs_total - 1), (r + 1) % 2)
        pvs = pv(pending_pair(j), (r - 2) % ns)
        old = [(stat_ref[h, 0], stat_ref[h, 1], stat_ref[h, 2]) for h in heads]
        for h in heads:
            m, a1, a2 = old[h]
            on = [sel_ref[h, pl.ds(j * ATTN_PAIR + b, 1), :] > 0.0 for b in range(ATTN_PAIR)]
            m_new = m
            for b in range(ATTN_PAIR):
                m_new = jnp.maximum(m_new, jnp.where(on[b], smax_ref[r % 2][h][b:b + 1, :], NEG))
            for b in range(ATTN_PAIR):
                rows = slice(b * MOBA_BLOCK, (b + 1) * MOBA_BLOCK)
                shift = jnp.where(on[b], m_new, -NEG)
                p_ref[r][h][rows, :] = jnp.exp2((s_ref[r % 2][h][rows, :] - shift).astype(bf16))
            stat_ref[h, 0], stat_ref[h, 1], stat_ref[h, 2] = m_new, jnp.exp2(m - m_new), a1
            acc_ref[h][...] = a2 * acc_ref[h][...] + pvs[h]

    def trip(t, carry):
        for r in range(ns):
            stage(ns * t + r, r)
        return carry

    n_pairs = (i + ATTN_PAIR - 1) // ATTN_PAIR
    n_trips = n_pairs // ns
    done = n_trips * ns
    lax.fori_loop(0, n_trips, trip, 0)
    for r in range(ns):
        @pl.when(n_pairs - done == r)
        def _():
            for k in range(r):
                stage(done + k, k)
            pv2 = pv(pending_pair(n_pairs), (r - 2) % ns)
            pv1 = pv(pending_pair(n_pairs + 1), (r - 1) % ns)
            for h in heads:
                a1, a2 = stat_ref[h, 1], stat_ref[h, 2]
                acc = a1 * (a2 * acc_ref[h][...] + pv2[h]) + pv1[h]
                out = acc[:HEAD_DIM] / acc[HEAD_DIM:HEAD_DIM + 1]
                o_ref[:, hsl[h]] = out.T.astype(o_ref.dtype)


def _moba_attention(proj3):
    b, s, _ = proj3.shape
    n_blocks = s // MOBA_BLOCK
    width = ATTN_HEADS * HEAD_DIM
    head_groups = N_HEADS // ATTN_HEADS
    return pl.pallas_call(
        functools.partial(_moba_kernel, n_blocks=n_blocks),
        out_shape=jax.ShapeDtypeStruct((b, s, N_HEADS * HEAD_DIM), jnp.bfloat16),
        grid=(b, head_groups, n_blocks),
        in_specs=[
            pl.BlockSpec((None, MOBA_BLOCK, width), lambda bi, h, i: (bi, i, h)),
            pl.BlockSpec((None, s, width), lambda bi, h, i: (bi, 0, head_groups + h)),
            pl.BlockSpec((None, s, width), lambda bi, h, i: (bi, 0, 2 * head_groups + h)),
        ],
        out_specs=pl.BlockSpec((None, MOBA_BLOCK, width), lambda bi, h, i: (bi, i, h)),
        scratch_shapes=[
            pltpu.VMEM((ATTN_HEADS, VT_ROWS, s), jnp.bfloat16),
            pltpu.VMEM((ATTN_HEADS, n_blocks, HEAD_DIM), jnp.float32),
            pltpu.VMEM((ATTN_HEADS, n_blocks, MOBA_BLOCK), jnp.float32),
            pltpu.VMEM((ATTN_HEADS, HEAD_DIM, MOBA_BLOCK), jnp.bfloat16),
            pltpu.VMEM((ATTN_HEADS, 3, 1, MOBA_BLOCK), jnp.float32),
        ] + [pltpu.VMEM((ATTN_PAIR * MOBA_BLOCK, MOBA_BLOCK), jnp.float32)] * (2 * ATTN_HEADS)
          + [pltpu.VMEM((ATTN_PAIR, MOBA_BLOCK), jnp.float32)] * (2 * ATTN_HEADS)
          + [pltpu.VMEM((ATTN_PAIR * MOBA_BLOCK, MOBA_BLOCK), jnp.bfloat16)] * (ATTN_SLOTS * ATTN_HEADS)
          + [pltpu.VMEM((VT_ROWS, MOBA_BLOCK), jnp.float32)] * ATTN_HEADS,
        compiler_params=pltpu.CompilerParams(
            dimension_semantics=("arbitrary", "arbitrary", "arbitrary"), vmem_limit_bytes=VMEM_LIMIT_BYTES),
        name="moba_attn",
    )(proj3, proj3, proj3)


def _mix_kernel(x_ref, attn_ref, cg_ref, bg_ref, xc_ref, ga_ref, gc_ref, cgh_ref, xch_ref,
                cw_ref, wab_ref, wcb_ref, wout_ref, o_ref, *, tiles_per_seq):
    f32 = jnp.float32
    i = pl.program_id(0)
    cx = cg_ref[...].astype(f32) * xc_ref[...].astype(f32)
    live = (i % tiles_per_seq != 0).astype(f32)
    halo = cgh_ref[...].astype(f32) * xch_ref[...].astype(f32) * live
    row = lax.broadcasted_iota(jnp.int32, cx.shape, 0)
    cw = cw_ref[...]
    z = cw[CONV_K - 1:CONV_K, :] * cx
    for d in range(1, CONV_K):
        shifted = pltpu.roll(cx, d, 0)
        for r in range(d):
            shifted = jnp.where(row == r, halo[HALO - d + r:HALO - d + r + 1, :], shifted)
        z = z + cw[CONV_K - 1 - d:CONV_K - d, :] * shifted
    yc = jnp.dot((bg_ref[...].astype(f32) * z).astype(jnp.bfloat16), wcb_ref[...], preferred_element_type=f32)
    ya = jnp.dot(attn_ref[...], wab_ref[...], preferred_element_type=f32)
    merged = jax.nn.sigmoid(ga_ref[...].astype(f32)) * ya + jax.nn.sigmoid(gc_ref[...].astype(f32)) * yc
    o_ref[...] = x_ref[...] + jnp.dot(merged.astype(jnp.bfloat16), wout_ref[...], preferred_element_type=f32)


def _mix(x2, attn2, proj, conv_w, wab, wcb, wout, seq):
    t = x2.shape[0]
    tiles_per_seq = seq // MIX_TM
    halo_per_tile = MIX_TM // HALO

    def col(c):
        return pl.BlockSpec((MIX_TM, D_MODEL), lambda i: (i, c))

    def halo(c):
        return pl.BlockSpec((HALO, D_MODEL), lambda i: (jnp.maximum(i * halo_per_tile - 1, 0), c))

    wspec = pl.BlockSpec((D_MODEL, D_MODEL), lambda i: (0, 0))
    return pl.pallas_call(
        functools.partial(_mix_kernel, tiles_per_seq=tiles_per_seq),
        out_shape=jax.ShapeDtypeStruct((t, D_MODEL), jnp.float32),
        grid=(t // MIX_TM,),
        in_specs=[
            pl.BlockSpec((MIX_TM, D_MODEL), lambda i: (i, 0)),
            pl.BlockSpec((MIX_TM, D_MODEL), lambda i: (i, 0)),
            col(3), col(4), col(5), col(6), col(7),
            halo(3), halo(5),
            pl.BlockSpec((CONV_K, D_MODEL), lambda i: (0, 0)),
            wspec, wspec, wspec,
        ],
        out_specs=pl.BlockSpec((MIX_TM, D_MODEL), lambda i: (i, 0)),
        compiler_params=pltpu.CompilerParams(
            dimension_semantics=("arbitrary",), vmem_limit_bytes=VMEM_LIMIT_BYTES),
        name="mix",
    )(x2, attn2, proj, proj, proj, proj, proj, proj, proj, conv_w, wab, wcb, wout)


def _ffn_kernel(h_ref, g_ref, wgu_ref, wd_ref, gf_ref, o_ref):
    f32 = jnp.float32
    h = h_ref[...]
    u = _rms(h, g_ref[...]).astype(jnp.bfloat16)
    ck = D_FF // FFN_CHUNKS
    acc = h
    for c in range(FFN_CHUNKS):
        gate = jnp.dot(u, wgu_ref[:, c * ck:(c + 1) * ck], preferred_element_type=f32)
        up = jnp.dot(u, wgu_ref[:, D_FF + c * ck:D_FF + (c + 1) * ck], preferred_element_type=f32)
        act = (gate * jax.nn.sigmoid(gate) * up).astype(jnp.bfloat16)
        acc = acc + jnp.dot(act, wd_ref[c * ck:(c + 1) * ck, :], preferred_element_type=f32)
    o_ref[...] = _rms(acc, gf_ref[...])


def _ffn(h1, g_ffn, wgu, wd, g_final):
    t = h1.shape[0]
    return pl.pallas_call(
        _ffn_kernel,
        out_shape=jax.ShapeDtypeStruct((t, D_MODEL), jnp.float32),
        grid=(t // FFN_TM,),
        in_specs=[
            pl.BlockSpec((FFN_TM, D_MODEL), lambda i: (i, 0)),
            pl.BlockSpec((1, D_MODEL), lambda i: (0, 0)),
            pl.BlockSpec((D_MODEL, 2 * D_FF), lambda i: (0, 0), pipeline_mode=pl.Buffered(1)),
            pl.BlockSpec((D_FF, D_MODEL), lambda i: (0, 0), pipeline_mode=pl.Buffered(1)),
            pl.BlockSpec((1, D_MODEL), lambda i: (0, 0)),
        ],
        out_specs=pl.BlockSpec((FFN_TM, D_MODEL), lambda i: (i, 0)),
        compiler_params=pltpu.CompilerParams(
            dimension_semantics=("arbitrary",), vmem_limit_bytes=VMEM_LIMIT_BYTES),
        name="ffn",
    )(h1, g_ffn, wgu, wd, g_final)


def _rope_tables(seq):
    inv_freq = ROPE_THETA ** (-jnp.arange(0, ROPE_DIM, 2, dtype=jnp.float32) / ROPE_DIM)
    freq = jnp.concatenate([inv_freq, inv_freq, jnp.zeros((HEAD_DIM - ROPE_DIM,), jnp.float32)])
    ang = jnp.arange(seq, dtype=jnp.int32).astype(jnp.float32)[:, None] * freq[None, :]
    sin = jnp.sin(ang)
    first = jnp.asarray(np.arange(HEAD_DIM) < ROPE_HALF)[None, :]
    return jnp.stack([jnp.cos(ang), jnp.where(first, -sin, 0.0), jnp.where(first, 0.0, sin)])


def kernel(x, g_mix, w_in, conv_w, w_attn_branch, w_conv_branch, w_out, g_ffn, w_gate_up, w_down, g_final):
    b, s, d = x.shape
    depth = w_in.shape[0]
    assert d == D_MODEL and s % PROJ_TM == 0 and s % (ATTN_PAIR * MOBA_BLOCK) == 0
    bf16 = jnp.bfloat16
    tables = _rope_tables(s)
    h = x.reshape(b * s, d)
    for l in range(depth):
        proj = _in_proj(h, g_mix[l][None, :], w_in[l], tables, s)
        attn = _moba_attention(proj.reshape(b, s, IN_WIDTH))
        h = _mix(h, attn.reshape(b * s, d), proj, conv_w[l], w_attn_branch[l].astype(bf16),
                 w_conv_branch[l].astype(bf16), w_out[l].astype(bf16), s)
        last = l == depth - 1
        assert last, "only DEPTH == 1 is supported"
        h = _ffn(h, g_ffn[l][None, :], w_gate_up[l].astype(bf16), w_down[l].astype(bf16), g_final[None, :])
    return h.reshape(b, s, d)
```

```python
import functools
import math

import jax
import jax.numpy as jnp
import numpy as np
from jax import lax
from jax.experimental import pallas as pl
from jax.experimental.pallas import tpu as pltpu

D_MODEL = 1024
N_HEADS = 8
HEAD_DIM = 128
MOBA_BLOCK = 256
MOBA_TOPK = 3
ROPE_THETA = 500000.0
ROPE_DIM = HEAD_DIM // 4
ROPE_HALF = ROPE_DIM // 2
CONV_K = 3
D_FF = 2816
EPS = 1e-6
NEG = -1e30
N_SPLITS = 8
IN_WIDTH = N_SPLITS * D_MODEL

VMEM_LIMIT_BYTES = 56 * 1024 * 1024
PROJ_TM = 2048
PROJ_SUB = 256
MIX_TM = 512
FFN_TM = 512
FFN_CHUNKS = 1
ATTN_HEADS = 4
ATTN_PAIR = 2
ATTN_SLOTS = 4
SUBLANES = 8
VT_ROWS = HEAD_DIM + 2 * SUBLANES
HALO = SUBLANES


def _rms(xf, g):
    return xf * lax.rsqrt(jnp.mean(xf * xf, axis=-1, keepdims=True) + EPS) * g


def _in_proj_kernel(x_ref, g_ref, w_ref, cos_ref, sina_ref, sinb_ref, o_ref, u_ref):
    j = pl.program_id(1)

    @pl.when(j == 0)
    def _():
        u_ref[...] = _rms(x_ref[...], g_ref[...]).astype(jnp.bfloat16)

    def pieces(store):
        for c in range(D_MODEL // PROJ_SUB):
            w = w_ref[:, c * PROJ_SUB:(c + 1) * PROJ_SUB].astype(jnp.bfloat16)
            store(jnp.dot(u_ref[...], w, preferred_element_type=jnp.float32), c * PROJ_SUB)

    def rope_store(acc, col0):
        scale = jnp.where(j == 0, math.log2(math.e) / math.sqrt(HEAD_DIM), 1.0)
        cos, sina, sinb = cos_ref[...] * scale, sina_ref[...] * scale, sinb_ref[...] * scale
        for h in range(PROJ_SUB // HEAD_DIM):
            t = acc[:, h * HEAD_DIM:(h + 1) * HEAD_DIM]
            r = (t * cos + pltpu.roll(t, HEAD_DIM - ROPE_HALF, 1) * sina
                 + pltpu.roll(t, ROPE_HALF, 1) * sinb)
            o_ref[:, col0 + h * HEAD_DIM:col0 + (h + 1) * HEAD_DIM] = r.astype(o_ref.dtype)

    def plain_store(acc, col0):
        o_ref[:, col0:col0 + PROJ_SUB] = acc.astype(o_ref.dtype)

    @pl.when(j < 2)
    def _():
        pieces(rope_store)

    @pl.when(j >= 2)
    def _():
        pieces(plain_store)


def _in_proj(x2, g_mix, w_in, tables, seq):
    t = x2.shape[0]
    tiles_per_seq = seq // PROJ_TM

    def tab_spec(k):
        return pl.BlockSpec((None, PROJ_TM, HEAD_DIM), lambda i, j: (k, i % tiles_per_seq, 0))

    return pl.pallas_call(
        _in_proj_kernel,
        out_shape=jax.ShapeDtypeStruct((t, IN_WIDTH), jnp.bfloat16),
        grid=(t // PROJ_TM, N_SPLITS),
        in_specs=[
            pl.BlockSpec((PROJ_TM, D_MODEL), lambda i, j: (i, 0)),
            pl.BlockSpec((1, D_MODEL), lambda i, j: (0, 0)),
            pl.BlockSpec((D_MODEL, D_MODEL), lambda i, j: (0, j)),
            tab_spec(0), tab_spec(1), tab_spec(2),
        ],
        out_specs=pl.BlockSpec((PROJ_TM, D_MODEL), lambda i, j: (i, j)),
        scratch_shapes=[pltpu.VMEM((PROJ_TM, D_MODEL), jnp.bfloat16)],
        compiler_params=pltpu.CompilerParams(
            dimension_semantics=("arbitrary", "arbitrary"), vmem_limit_bytes=VMEM_LIMIT_BYTES),
        name="in_proj",
    )(x2, g_mix, w_in, tables, tables, tables)


def _reduce_rows(x, op, reduce_fn):
    rows = x.shape[0]
    while rows > SUBLANES and rows % (2 * SUBLANES) == 0:
        rows //= 2
        x = op(x[:rows], x[rows:])
    return reduce_fn(x, axis=0, keepdims=True)


def _moba_kernel(q_ref, k_ref, v_ref, o_ref, vt_ref, kmean_ref, sel_ref, qt_ref, stat_ref, *slot_refs, n_blocks):
    i = pl.program_id(2)
    f32, bf16 = jnp.float32, jnp.bfloat16
    heads = range(ATTN_HEADS)
    hsl = [slice(h * HEAD_DIM, (h + 1) * HEAD_DIM) for h in heads]
    nh, ns, gk = ATTN_HEADS, ATTN_SLOTS, ATTN_PAIR * MOBA_BLOCK
    n_pairs_total = n_blocks // ATTN_PAIR
    groups = [slot_refs[g * nh:(g + 1) * nh] for g in range(len(slot_refs) // nh)]
    s_ref = groups[0:2]
    smax_ref = groups[2:4]
    p_ref = groups[4:4 + ns]
    acc_ref = groups[4 + ns]
    pown_ref = groups[5 + ns]

    @pl.when(i == 0)
    def _():
        row = lax.broadcasted_iota(jnp.int32, (VT_ROWS - HEAD_DIM, vt_ref.shape[2]), 0)
        for h in heads:
            vt_ref[h, HEAD_DIM:, :] = jnp.where(row == 0, 1.0, 0.0).astype(bf16)
            for c in range(n_blocks):
                rows = slice(c * MOBA_BLOCK, (c + 1) * MOBA_BLOCK)
                kmean_ref[h, c:c + 1, :] = jnp.mean(k_ref[rows, hsl[h]].astype(f32), axis=0, keepdims=True)
                vt_ref[h, :HEAD_DIM, rows] = v_ref[rows, hsl[h]].astype(f32).T.astype(bf16)
            for slot in range(ns):
                p_ref[slot][h][...] = jnp.zeros(p_ref[slot][h].shape, bf16)

    def krows(idx, size):
        return pl.ds(pl.multiple_of(idx * size, size), size)

    def qk(pair, slot):
        for h in heads:
            s = jnp.dot(k_ref[krows(pair, gk), hsl[h]], qt_ref[h], preferred_element_type=f32)
            s_ref[slot][h][...] = s
            for b in range(ATTN_PAIR):
                blk_s = s[b * MOBA_BLOCK:(b + 1) * MOBA_BLOCK]
                smax_ref[slot][h][b:b + 1, :] = _reduce_rows(blk_s, jnp.maximum, jnp.max)

    def pv(pair, slot):
        return [jnp.dot(vt_ref[h, :, krows(pair, gk)], p_ref[slot][h][...], preferred_element_type=f32)
                for h in heads]

    for h in heads:
        qt_ref[h] = q_ref[:, hsl[h]].astype(f32).T.astype(bf16)
    gates = [jnp.dot(kmean_ref[h].astype(bf16), qt_ref[h], preferred_element_type=f32) for h in heads]
    qk(0, 0)

    for h in heads:
        blk = lax.broadcasted_iota(jnp.int32, gates[h].shape, 0)
        past = blk < i
        g = jnp.where(past, gates[h], NEG)
        sel = jnp.zeros(g.shape, f32)
        for _ in range(MOBA_TOPK):
            mx = jnp.max(g, axis=0, keepdims=True)
            first = jnp.min(jnp.where(g == mx, blk, n_blocks), axis=0, keepdims=True)
            pick = blk == first
            sel = jnp.where(pick, 1.0, sel)
            g = jnp.where(pick, -jnp.inf, g)
        sel_ref[h] = jnp.where(past, sel, 0.0)

        acc_ref[h][...] = jnp.zeros(acc_ref[h].shape, f32)
        start = jnp.full((1, MOBA_BLOCK), NEG, f32)
        stat_ref[h, 0], stat_ref[h, 1], stat_ref[h, 2] = start, jnp.ones_like(start), jnp.ones_like(start)

    def pending(j, pvs):
        return [jnp.where(j >= 2, x, 0.0) for x in pvs]

    def stage(j, r):
        qk(jnp.minimum(j + 1, n_pairs_total - 1), (r + 1) % 2)
        pvs = pending(j, pv(jnp.maximum(j - 2, 0), (r - 2) % ns))
        old = [(stat_ref[h, 0], stat_ref[h, 1], stat_ref[h, 2]) for h in heads]
        for h in heads:
            m, a1, a2 = old[h]
            on = [sel_ref[h, pl.ds(j * ATTN_PAIR + b, 1), :] > 0.0 for b in range(ATTN_PAIR)]
            m_new = m
            for b in range(ATTN_PAIR):
                m_new = jnp.maximum(m_new, jnp.where(on[b], smax_ref[r % 2][h][b:b + 1, :], NEG))
            for b in range(ATTN_PAIR):
                rows = slice(b * MOBA_BLOCK, (b + 1) * MOBA_BLOCK)
                shift = jnp.where(on[b], m_new, -NEG)
                p_ref[r][h][rows, :] = jnp.exp2((s_ref[r % 2][h][rows, :] - shift).astype(bf16))
            stat_ref[h, 0], stat_ref[h, 1], stat_ref[h, 2] = m_new, jnp.exp2(m - m_new), a1
            acc_ref[h][...] = a2 * acc_ref[h][...] + pvs[h]

    def trip(t, carry):
        for r in range(ns):
            stage(ns * t + r, r)
        return carry

    n_pairs = (i + ATTN_PAIR - 1) // ATTN_PAIR
    n_trips = n_pairs // ns
    done = n_trips * ns
    lax.fori_loop(0, n_trips, trip, 0)
    for r in range(ns):
        @pl.when(n_pairs - done == r)
        def _():
            own_s = [jnp.dot(k_ref[krows(i, MOBA_BLOCK), hsl[h]], qt_ref[h], preferred_element_type=f32)
                     for h in heads]
            for k in range(r):
                stage(done + k, k)
            pv2 = pending(n_pairs, pv(jnp.maximum(n_pairs - 2, 0), (r - 2) % ns))
            pv1 = pending(n_pairs + 1, pv(jnp.maximum(n_pairs - 1, 0), (r - 1) % ns))
            for h in heads:
                key_idx = lax.broadcasted_iota(jnp.int32, own_s[h].shape, 0)
                qry_idx = lax.broadcasted_iota(jnp.int32, own_s[h].shape, 1)
                st = jnp.where(key_idx <= qry_idx, own_s[h], NEG)
                m = stat_ref[h, 0]
                m_fin = jnp.maximum(m, _reduce_rows(st, jnp.maximum, jnp.max))
                pown_ref[h][...] = jnp.exp2((st - m_fin).astype(bf16))
                pv0 = jnp.dot(vt_ref[h, :, krows(i, MOBA_BLOCK)], pown_ref[h][...], preferred_element_type=f32)
                a1, a2 = stat_ref[h, 1], stat_ref[h, 2]
                acc = jnp.exp2(m - m_fin) * (a1 * (a2 * acc_ref[h][...] + pv2[h]) + pv1[h]) + pv0
                out = acc[:HEAD_DIM] / acc[HEAD_DIM:HEAD_DIM + 1]
                o_ref[:, hsl[h]] = out.T.astype(o_ref.dtype)


def _moba_attention(proj3):
    b, s, _ = proj3.shape
    n_blocks = s // MOBA_BLOCK
    width = ATTN_HEADS * HEAD_DIM
    head_groups = N_HEADS // ATTN_HEADS
    return pl.pallas_call(
        functools.partial(_moba_kernel, n_blocks=n_blocks),
        out_shape=jax.ShapeDtypeStruct((b, s, N_HEADS * HEAD_DIM), jnp.bfloat16),
        grid=(b, head_groups, n_blocks),
        in_specs=[
            pl.BlockSpec((None, MOBA_BLOCK, width), lambda bi, h, i: (bi, i, h)),
            pl.BlockSpec((None, s, width), lambda bi, h, i: (bi, 0, head_groups + h)),
            pl.BlockSpec((None, s, width), lambda bi, h, i: (bi, 0, 2 * head_groups + h)),
        ],
        out_specs=pl.BlockSpec((None, MOBA_BLOCK, width), lambda bi, h, i: (bi, i, h)),
        scratch_shapes=[
            pltpu.VMEM((ATTN_HEADS, VT_ROWS, s), jnp.bfloat16),
            pltpu.VMEM((ATTN_HEADS, n_blocks, HEAD_DIM), jnp.float32),
            pltpu.VMEM((ATTN_HEADS, n_blocks, MOBA_BLOCK), jnp.float32),
            pltpu.VMEM((ATTN_HEADS, HEAD_DIM, MOBA_BLOCK), jnp.bfloat16),
            pltpu.VMEM((ATTN_HEADS, 3, 1, MOBA_BLOCK), jnp.float32),
        ] + [pltpu.VMEM((ATTN_PAIR * MOBA_BLOCK, MOBA_BLOCK), jnp.float32)] * (2 * ATTN_HEADS)
          + [pltpu.VMEM((ATTN_PAIR, MOBA_BLOCK), jnp.float32)] * (2 * ATTN_HEADS)
          + [pltpu.VMEM((ATTN_PAIR * MOBA_BLOCK, MOBA_BLOCK), jnp.bfloat16)] * (ATTN_SLOTS * ATTN_HEADS)
          + [pltpu.VMEM((VT_ROWS, MOBA_BLOCK), jnp.float32)] * ATTN_HEADS
          + [pltpu.VMEM((MOBA_BLOCK, MOBA_BLOCK), jnp.bfloat16)] * ATTN_HEADS,
        compiler_params=pltpu.CompilerParams(
            dimension_semantics=("arbitrary", "arbitrary", "arbitrary"), vmem_limit_bytes=VMEM_LIMIT_BYTES),
        name="moba_attn",
    )(proj3, proj3, proj3)


def _mix_kernel(x_ref, attn_ref, cg_ref, bg_ref, xc_ref, ga_ref, gc_ref, cgh_ref, xch_ref,
                cw_ref, wab_ref, wcb_ref, wout_ref, o_ref, *, tiles_per_seq):
    f32 = jnp.float32
    i = pl.program_id(0)
    cx = cg_ref[...].astype(f32) * xc_ref[...].astype(f32)
    live = (i % tiles_per_seq != 0).astype(f32)
    halo = cgh_ref[...].astype(f32) * xch_ref[...].astype(f32) * live
    row = lax.broadcasted_iota(jnp.int32, cx.shape, 0)
    cw = cw_ref[...]
    z = cw[CONV_K - 1:CONV_K, :] * cx
    for d in range(1, CONV_K):
        shifted = pltpu.roll(cx, d, 0)
        for r in range(d):
            shifted = jnp.where(row == r, halo[HALO - d + r:HALO - d + r + 1, :], shifted)
        z = z + cw[CONV_K - 1 - d:CONV_K - d, :] * shifted
    yc = jnp.dot((bg_ref[...].astype(f32) * z).astype(jnp.bfloat16), wcb_ref[...], preferred_element_type=f32)
    ya = jnp.dot(attn_ref[...], wab_ref[...], preferred_element_type=f32)
    merged = jax.nn.sigmoid(ga_ref[...].astype(f32)) * ya + jax.nn.sigmoid(gc_ref[...].astype(f32)) * yc
    o_ref[...] = x_ref[...] + jnp.dot(merged.astype(jnp.bfloat16), wout_ref[...], preferred_element_type=f32)


def _mix(x2, attn2, proj, conv_w, wab, wcb, wout, seq):
    t = x2.shape[0]
    tiles_per_seq = seq // MIX_TM
    halo_per_tile = MIX_TM // HALO

    def col(c):
        return pl.BlockSpec((MIX_TM, D_MODEL), lambda i: (i, c))

    def halo(c):
        return pl.BlockSpec((HALO, D_MODEL), lambda i: (jnp.maximum(i * halo_per_tile - 1, 0), c))

    wspec = pl.BlockSpec((D_MODEL, D_MODEL), lambda i: (0, 0))
    return pl.pallas_call(
        functools.partial(_mix_kernel, tiles_per_seq=tiles_per_seq),
        out_shape=jax.ShapeDtypeStruct((t, D_MODEL), jnp.float32),
        grid=(t // MIX_TM,),
        in_specs=[
            pl.BlockSpec((MIX_TM, D_MODEL), lambda i: (i, 0)),
            pl.BlockSpec((MIX_TM, D_MODEL), lambda i: (i, 0)),
            col(3), col(4), col(5), col(6), col(7),
            halo(3), halo(5),
            pl.BlockSpec((CONV_K, D_MODEL), lambda i: (0, 0)),
            wspec, wspec, wspec,
        ],
        out_specs=pl.BlockSpec((MIX_TM, D_MODEL), lambda i: (i, 0)),
        compiler_params=pltpu.CompilerParams(
            dimension_semantics=("arbitrary",), vmem_limit_bytes=VMEM_LIMIT_BYTES),
        name="mix",
    )(x2, attn2, proj, proj, proj, proj, proj, proj, proj, conv_w, wab, wcb, wout)


def _ffn_kernel(h_ref, g_ref, wgu_ref, wd_ref, gf_ref, o_ref):
    f32 = jnp.float32
    h = h_ref[...]
    u = _rms(h, g_ref[...]).astype(jnp.bfloat16)
    ck = D_FF // FFN_CHUNKS
    acc = h
    for c in range(FFN_CHUNKS):
        gate = jnp.dot(u, wgu_ref[:, c * ck:(c + 1) * ck], preferred_element_type=f32)
        up = jnp.dot(u, wgu_ref[:, D_FF + c * ck:D_FF + (c + 1) * ck], preferred_element_type=f32)
        act = (gate * jax.nn.sigmoid(gate) * up).astype(jnp.bfloat16)
        acc = acc + jnp.dot(act, wd_ref[c * ck:(c + 1) * ck, :], preferred_element_type=f32)
    o_ref[...] = _rms(acc, gf_ref[...])


def _ffn(h1, g_ffn, wgu, wd, g_final):
    t = h1.shape[0]
    return pl.pallas_call(
        _ffn_kernel,
        out_shape=jax.ShapeDtypeStruct((t, D_MODEL), jnp.float32),
        grid=(t // FFN_TM,),
        in_specs=[
            pl.BlockSpec((FFN_TM, D_MODEL), lambda i: (i, 0)),
            pl.BlockSpec((1, D_MODEL), lambda i: (0, 0)),
            pl.BlockSpec((D_MODEL, 2 * D_FF), lambda i: (0, 0), pipeline_mode=pl.Buffered(1)),
            pl.BlockSpec((D_FF, D_MODEL), lambda i: (0, 0), pipeline_mode=pl.Buffered(1)),
            pl.BlockSpec((1, D_MODEL), lambda i: (0, 0)),
        ],
        out_specs=pl.BlockSpec((FFN_TM, D_MODEL), lambda i: (i, 0)),
        compiler_params=pltpu.CompilerParams(
            dimension_semantics=("arbitrary",), vmem_limit_bytes=VMEM_LIMIT_BYTES),
        name="ffn",
    )(h1, g_ffn, wgu, wd, g_final)


def _rope_tables(seq):
    inv_freq = ROPE_THETA ** (-jnp.arange(0, ROPE_DIM, 2, dtype=jnp.float32) / ROPE_DIM)
    freq = jnp.concatenate([inv_freq, inv_freq, jnp.zeros((HEAD_DIM - ROPE_DIM,), jnp.float32)])
    ang = jnp.arange(seq, dtype=jnp.int32).astype(jnp.float32)[:, None] * freq[None, :]
    sin = jnp.sin(ang)
    first = jnp.asarray(np.arange(HEAD_DIM) < ROPE_HALF)[None, :]
    return jnp.stack([jnp.cos(ang), jnp.where(first, -sin, 0.0), jnp.where(first, 0.0, sin)])


def kernel(x, g_mix, w_in, conv_w, w_attn_branch, w_conv_branch, w_out, g_ffn, w_gate_up, w_down, g_final):
    b, s, d = x.shape
    depth = w_in.shape[0]
    assert d == D_MODEL and s % PROJ_TM == 0 and s % (ATTN_PAIR * MOBA_BLOCK) == 0
    bf16 = jnp.bfloat16
    tables = _rope_tables(s)
    h = x.reshape(b * s, d)
    for l in range(depth):
        proj = _in_proj(h, g_mix[l][None, :], w_in[l], tables, s)
        attn = _moba_attention(proj.reshape(b, s, IN_WIDTH))
        h = _mix(h, attn.reshape(b * s, d), proj, conv_w[l], w_attn_branch[l].astype(bf16),
                 w_conv_branch[l].astype(bf16), w_out[l].astype(bf16), s)
        last = l == depth - 1
        assert last, "only DEPTH == 1 is supported"
        h = _ffn(h, g_ffn[l][None, :], w_gate_up[l].astype(bf16), w_down[l].astype(bf16), g_final[None, :])
    return h.reshape(b, s, d)
```

```python
import functools
import math

import jax
import jax.numpy as jnp
import numpy as np
from jax import lax
from jax.experimental import pallas as pl
from jax.experimental.pallas import tpu as pltpu

D_MODEL = 1024
N_HEADS = 8
HEAD_DIM = 128
MOBA_BLOCK = 256
MOBA_TOPK = 3
ROPE_THETA = 500000.0
ROPE_DIM = HEAD_DIM // 4
ROPE_HALF = ROPE_DIM // 2
CONV_K = 3
D_FF = 2816
EPS = 1e-6
NEG = -1e30
N_SPLITS = 8
IN_WIDTH = N_SPLITS * D_MODEL

VMEM_LIMIT_BYTES = 56 * 1024 * 1024
PROJ_TM = 2048
PROJ_SUB = 256
MIX_TM = 512
FFN_TM = 512
FFN_CHUNKS = 1
ATTN_HEADS = 4
ATTN_PAIR = 2
ATTN_SLOTS = 4
SUBLANES = 8
VT_ROWS = HEAD_DIM + 2 * SUBLANES
HALO = SUBLANES


def _rms(xf, g):
    return xf * lax.rsqrt(jnp.mean(xf * xf, axis=-1, keepdims=True) + EPS) * g


def _in_proj_kernel(x_ref, g_ref, w_ref, cosb_ref, sinb_ref, rot_ref, o_ref, u_ref):
    j = pl.program_id(1)

    @pl.when(j == 0)
    def _():
        u_ref[...] = _rms(x_ref[...], g_ref[...]).astype(jnp.bfloat16)

    def pieces(store):
        for c in range(D_MODEL // PROJ_SUB):
            w = w_ref[:, c * PROJ_SUB:(c + 1) * PROJ_SUB].astype(jnp.bfloat16)
            store(jnp.dot(u_ref[...], w, preferred_element_type=jnp.float32), c * PROJ_SUB)

    def rope_store(acc, col0):
        scale = jnp.where(j == 0, math.log2(math.e) / math.sqrt(HEAD_DIM), 1.0)
        cb, sb, ca, sa = cosb_ref[...], sinb_ref[...], rot_ref[0:1, :] * scale, rot_ref[1:2, :] * scale
        cos, sin = cb * ca - sb * sa, sb * ca + cb * sa
        lane = lax.broadcasted_iota(jnp.int32, (1, HEAD_DIM), 1)
        sina = jnp.where(lane < ROPE_HALF, -sin, 0.0)
        sinb = jnp.where(lane < ROPE_HALF, 0.0, sin)
        for h in range(PROJ_SUB // HEAD_DIM):
            t = acc[:, h * HEAD_DIM:(h + 1) * HEAD_DIM]
            r = (t * cos + pltpu.roll(t, HEAD_DIM - ROPE_HALF, 1) * sina
                 + pltpu.roll(t, ROPE_HALF, 1) * sinb)
            o_ref[:, col0 + h * HEAD_DIM:col0 + (h + 1) * HEAD_DIM] = r.astype(o_ref.dtype)

    def plain_store(acc, col0):
        o_ref[:, col0:col0 + PROJ_SUB] = acc.astype(o_ref.dtype)

    @pl.when(j < 2)
    def _():
        pieces(rope_store)

    @pl.when(j >= 2)
    def _():
        pieces(plain_store)


def _in_proj(x2, g_mix, w_in, tables, seq):
    t = x2.shape[0]
    tiles_per_seq = seq // PROJ_TM
    base, rot = tables

    def tab_spec(k):
        return pl.BlockSpec((None, PROJ_TM, HEAD_DIM), lambda i, j: (k, 0, 0))

    return pl.pallas_call(
        _in_proj_kernel,
        out_shape=jax.ShapeDtypeStruct((t, IN_WIDTH), jnp.bfloat16),
        grid=(t // PROJ_TM, N_SPLITS),
        in_specs=[
            pl.BlockSpec((PROJ_TM, D_MODEL), lambda i, j: (i, 0)),
            pl.BlockSpec((1, D_MODEL), lambda i, j: (0, 0)),
            pl.BlockSpec((D_MODEL, D_MODEL), lambda i, j: (0, j)),
            tab_spec(0), tab_spec(1),
            pl.BlockSpec((None, 2, HEAD_DIM), lambda i, j: (i % tiles_per_seq, 0, 0)),
        ],
        out_specs=pl.BlockSpec((PROJ_TM, D_MODEL), lambda i, j: (i, j)),
        scratch_shapes=[pltpu.VMEM((PROJ_TM, D_MODEL), jnp.bfloat16)],
        compiler_params=pltpu.CompilerParams(
            dimension_semantics=("arbitrary", "arbitrary"), vmem_limit_bytes=VMEM_LIMIT_BYTES),
        name="in_proj",
    )(x2, g_mix, w_in, base, base, rot)


def _reduce_rows(x, op, reduce_fn):
    rows = x.shape[0]
    while rows > SUBLANES and rows % (2 * SUBLANES) == 0:
        rows //= 2
        x = op(x[:rows], x[rows:])
    return reduce_fn(x, axis=0, keepdims=True)


def _moba_kernel(q_ref, k_ref, v_ref, o_ref, vt_ref, kmean_ref, sel_ref, qt_ref, stat_ref, *slot_refs, n_blocks):
    i = pl.program_id(2)
    f32, bf16 = jnp.float32, jnp.bfloat16
    heads = range(ATTN_HEADS)
    hsl = [slice(h * HEAD_DIM, (h + 1) * HEAD_DIM) for h in heads]
    nh, ns, gk = ATTN_HEADS, ATTN_SLOTS, ATTN_PAIR * MOBA_BLOCK
    n_pairs_total = n_blocks // ATTN_PAIR
    groups = [slot_refs[g * nh:(g + 1) * nh] for g in range(len(slot_refs) // nh)]
    s_ref = groups[0:2]
    smax_ref = groups[2:4]
    p_ref = groups[4:4 + ns]
    acc_ref = groups[4 + ns]
    pown_ref = groups[5 + ns]

    @pl.when(i == 0)
    def _():
        row = lax.broadcasted_iota(jnp.int32, (VT_ROWS - HEAD_DIM, vt_ref.shape[2]), 0)
        for h in heads:
            vt_ref[h, HEAD_DIM:, :] = jnp.where(row == 0, 1.0, 0.0).astype(bf16)
            for c in range(n_blocks):
                rows = slice(c * MOBA_BLOCK, (c + 1) * MOBA_BLOCK)
                kmean_ref[h, c:c + 1, :] = jnp.mean(k_ref[rows, hsl[h]].astype(f32), axis=0, keepdims=True)
                vt_ref[h, :HEAD_DIM, rows] = v_ref[rows, hsl[h]].astype(f32).T.astype(bf16)
            for slot in range(ns):
                p_ref[slot][h][...] = jnp.zeros(p_ref[slot][h].shape, bf16)

    def krows(idx, size):
        return pl.ds(pl.multiple_of(idx * size, size), size)

    def qk(pair, slot):
        for h in heads:
            s = jnp.dot(k_ref[krows(pair, gk), hsl[h]], qt_ref[h], preferred_element_type=f32)
            s_ref[slot][h][...] = s
            for b in range(ATTN_PAIR):
                blk_s = s[b * MOBA_BLOCK:(b + 1) * MOBA_BLOCK]
                smax_ref[slot][h][b:b + 1, :] = _reduce_rows(blk_s, jnp.maximum, jnp.max)

    def pv(pair, slot):
        return [jnp.dot(vt_ref[h, :, krows(pair, gk)], p_ref[slot][h][...], preferred_element_type=f32)
                for h in heads]

    for h in heads:
        qt_ref[h] = q_ref[:, hsl[h]].astype(f32).T.astype(bf16)
    gates = [jnp.dot(kmean_ref[h].astype(bf16), qt_ref[h], preferred_element_type=f32) for h in heads]
    qk(0, 0)

    for h in heads:
        blk = lax.broadcasted_iota(jnp.int32, gates[h].shape, 0)
        past = blk < i
        g = jnp.where(past, gates[h], NEG)
        sel = jnp.zeros(g.shape, f32)
        for _ in range(MOBA_TOPK):
            mx = jnp.max(g, axis=0, keepdims=True)
            first = jnp.min(jnp.where(g == mx, blk, n_blocks), axis=0, keepdims=True)
            pick = blk == first
            sel = jnp.where(pick, 1.0, sel)
            g = jnp.where(pick, -jnp.inf, g)
        sel_ref[h] = jnp.where(past, sel, 0.0)

        acc_ref[h][...] = jnp.zeros(acc_ref[h].shape, f32)
        start = jnp.full((1, MOBA_BLOCK), NEG, f32)
        stat_ref[h, 0], stat_ref[h, 1], stat_ref[h, 2] = start, jnp.ones_like(start), jnp.ones_like(start)

    def pending(j, pvs):
        return [jnp.where(j >= 2, x, 0.0) for x in pvs]

    def stage(j, r):
        qk(jnp.minimum(j + 1, n_pairs_total - 1), (r + 1) % 2)
        pvs = pending(j, pv(jnp.maximum(j - 2, 0), (r - 2) % ns))
        old = [(stat_ref[h, 0], stat_ref[h, 1], stat_ref[h, 2]) for h in heads]
        for h in heads:
            m, a1, a2 = old[h]
            on = [sel_ref[h, pl.ds(j * ATTN_PAIR + b, 1), :] > 0.0 for b in range(ATTN_PAIR)]
            m_new = m
            for b in range(ATTN_PAIR):
                m_new = jnp.maximum(m_new, jnp.where(on[b], smax_ref[r % 2][h][b:b + 1, :], NEG))
            for b in range(ATTN_PAIR):
                rows = slice(b * MOBA_BLOCK, (b + 1) * MOBA_BLOCK)
                shift = jnp.where(on[b], m_new, -NEG)
                p_ref[r][h][rows, :] = jnp.exp2((s_ref[r % 2][h][rows, :] - shift).astype(bf16))
            stat_ref[h, 0], stat_ref[h, 1], stat_ref[h, 2] = m_new, jnp.exp2(m - m_new), a1
            acc_ref[h][...] = a2 * acc_ref[h][...] + pvs[h]

    def trip(t, carry):
        for r in range(ns):
            stage(ns * t + r, r)
        return carry

    n_pairs = (i + ATTN_PAIR - 1) // ATTN_PAIR
    n_trips = n_pairs // ns
    done = n_trips * ns
    lax.fori_loop(0, n_trips, trip, 0)
    for r in range(ns):
        @pl.when(n_pairs - done == r)
        def _():
            own_s = [jnp.dot(k_ref[krows(i, MOBA_BLOCK), hsl[h]], qt_ref[h], preferred_element_type=f32)
                     for h in heads]
            for k in range(r):
                stage(done + k, k)
            pv2 = pending(n_pairs, pv(jnp.maximum(n_pairs - 2, 0), (r - 2) % ns))
            pv1 = pending(n_pairs + 1, pv(jnp.maximum(n_pairs - 1, 0), (r - 1) % ns))
            for h in heads:
                key_idx = lax.broadcasted_iota(jnp.int32, own_s[h].shape, 0)
                qry_idx = lax.broadcasted_iota(jnp.int32, own_s[h].shape, 1)
                st = jnp.where(key_idx <= qry_idx, own_s[h], NEG)
                m = stat_ref[h, 0]
                m_fin = jnp.maximum(m, _reduce_rows(st, jnp.maximum, jnp.max))
                pown_ref[h][...] = jnp.exp2((st - m_fin).astype(bf16))
                pv0 = jnp.dot(vt_ref[h, :, krows(i, MOBA_BLOCK)], pown_ref[h][...], preferred_element_type=f32)
                a1, a2 = stat_ref[h, 1], stat_ref[h, 2]
                acc = jnp.exp2(m - m_fin) * (a1 * (a2 * acc_ref[h][...] + pv2[h]) + pv1[h]) + pv0
                out = acc[:HEAD_DIM] / acc[HEAD_DIM:HEAD_DIM + 1]
                o_ref[:, hsl[h]] = out.T.astype(o_ref.dtype)


def _moba_attention(proj3):
    b, s, _ = proj3.shape
    n_blocks = s // MOBA_BLOCK
    width = ATTN_HEADS * HEAD_DIM
    head_groups = N_HEADS // ATTN_HEADS
    return pl.pallas_call(
        functools.partial(_moba_kernel, n_blocks=n_blocks),
        out_shape=jax.ShapeDtypeStruct((b, s, N_HEADS * HEAD_DIM), jnp.bfloat16),
        grid=(b, head_groups, n_blocks),
        in_specs=[
            pl.BlockSpec((None, MOBA_BLOCK, width), lambda bi, h, i: (bi, i, h)),
            pl.BlockSpec((None, s, width), lambda bi, h, i: (bi, 0, head_groups + h)),
            pl.BlockSpec((None, s, width), lambda bi, h, i: (bi, 0, 2 * head_groups + h)),
        ],
        out_specs=pl.BlockSpec((None, MOBA_BLOCK, width), lambda bi, h, i: (bi, i, h)),
        scratch_shapes=[
            pltpu.VMEM((ATTN_HEADS, VT_ROWS, s), jnp.bfloat16),
            pltpu.VMEM((ATTN_HEADS, n_blocks, HEAD_DIM), jnp.float32),
            pltpu.VMEM((ATTN_HEADS, n_blocks, MOBA_BLOCK), jnp.float32),
            pltpu.VMEM((ATTN_HEADS, HEAD_DIM, MOBA_BLOCK), jnp.bfloat16),
            pltpu.VMEM((ATTN_HEADS, 3, 1, MOBA_BLOCK), jnp.float32),
        ] + [pltpu.VMEM((ATTN_PAIR * MOBA_BLOCK, MOBA_BLOCK), jnp.float32)] * (2 * ATTN_HEADS)
          + [pltpu.VMEM((ATTN_PAIR, MOBA_BLOCK), jnp.float32)] * (2 * ATTN_HEADS)
          + [pltpu.VMEM((ATTN_PAIR * MOBA_BLOCK, MOBA_BLOCK), jnp.bfloat16)] * (ATTN_SLOTS * ATTN_HEADS)
          + [pltpu.VMEM((VT_ROWS, MOBA_BLOCK), jnp.float32)] * ATTN_HEADS
          + [pltpu.VMEM((MOBA_BLOCK, MOBA_BLOCK), jnp.bfloat16)] * ATTN_HEADS,
        compiler_params=pltpu.CompilerParams(
            dimension_semantics=("arbitrary", "arbitrary", "arbitrary"), vmem_limit_bytes=VMEM_LIMIT_BYTES),
        name="moba_attn",
    )(proj3, proj3, proj3)


def _mix_kernel(x_ref, attn_ref, cg_ref, bg_ref, xc_ref, ga_ref, gc_ref, cgh_ref, xch_ref,
                cw_ref, wab_ref, wcb_ref, wout_ref, o_ref, *, tiles_per_seq):
    f32 = jnp.float32
    i = pl.program_id(0)
    cx = cg_ref[...].astype(f32) * xc_ref[...].astype(f32)
    live = (i % tiles_per_seq != 0).astype(f32)
    halo = cgh_ref[...].astype(f32) * xch_ref[...].astype(f32) * live
    row = lax.broadcasted_iota(jnp.int32, cx.shape, 0)
    cw = cw_ref[...]
    z = cw[CONV_K - 1:CONV_K, :] * cx
    for d in range(1, CONV_K):
        shifted = pltpu.roll(cx, d, 0)
        for r in range(d):
            shifted = jnp.where(row == r, halo[HALO - d + r:HALO - d + r + 1, :], shifted)
        z = z + cw[CONV_K - 1 - d:CONV_K - d, :] * shifted
    yc = jnp.dot((bg_ref[...].astype(f32) * z).astype(jnp.bfloat16), wcb_ref[...], preferred_element_type=f32)
    ya = jnp.dot(attn_ref[...], wab_ref[...], preferred_element_type=f32)
    merged = jax.nn.sigmoid(ga_ref[...].astype(f32)) * ya + jax.nn.sigmoid(gc_ref[...].astype(f32)) * yc
    o_ref[...] = x_ref[...] + jnp.dot(merged.astype(jnp.bfloat16), wout_ref[...], preferred_element_type=f32)


def _mix(x2, attn2, proj, conv_w, wab, wcb, wout, seq):
    t = x2.shape[0]
    tiles_per_seq = seq // MIX_TM
    halo_per_tile = MIX_TM // HALO

    def col(c):
        return pl.BlockSpec((MIX_TM, D_MODEL), lambda i: (i, c))

    def halo(c):
        return pl.BlockSpec((HALO, D_MODEL), lambda i: (jnp.maximum(i * halo_per_tile - 1, 0), c))

    wspec = pl.BlockSpec((D_MODEL, D_MODEL), lambda i: (0, 0))
    return pl.pallas_call(
        functools.partial(_mix_kernel, tiles_per_seq=tiles_per_seq),
        out_shape=jax.ShapeDtypeStruct((t, D_MODEL), jnp.float32),
        grid=(t // MIX_TM,),
        in_specs=[
            pl.BlockSpec((MIX_TM, D_MODEL), lambda i: (i, 0)),
            pl.BlockSpec((MIX_TM, D_MODEL), lambda i: (i, 0)),
            col(3), col(4), col(5), col(6), col(7),
            halo(3), halo(5),
            pl.BlockSpec((CONV_K, D_MODEL), lambda i: (0, 0)),
            wspec, wspec, wspec,
        ],
        out_specs=pl.BlockSpec((MIX_TM, D_MODEL), lambda i: (i, 0)),
        compiler_params=pltpu.CompilerParams(
            dimension_semantics=("arbitrary",), vmem_limit_bytes=VMEM_LIMIT_BYTES),
        name="mix",
    )(x2, attn2, proj, proj, proj, proj, proj, proj, proj, conv_w, wab, wcb, wout)


def _ffn_kernel(h_ref, g_ref, wgu_ref, wd_ref, gf_ref, o_ref):
    f32 = jnp.float32
    h = h_ref[...]
    u = _rms(h, g_ref[...]).astype(jnp.bfloat16)
    ck = D_FF // FFN_CHUNKS
    acc = h
    for c in range(FFN_CHUNKS):
        gate = jnp.dot(u, wgu_ref[:, c * ck:(c + 1) * ck], preferred_element_type=f32)
        up = jnp.dot(u, wgu_ref[:, D_FF + c * ck:D_FF + (c + 1) * ck], preferred_element_type=f32)
        act = (gate * jax.nn.sigmoid(gate) * up).astype(jnp.bfloat16)
        acc = acc + jnp.dot(act, wd_ref[c * ck:(c + 1) * ck, :], preferred_element_type=f32)
    o_ref[...] = _rms(acc, gf_ref[...])


def _ffn(h1, g_ffn, wgu, wd, g_final):
    t = h1.shape[0]
    return pl.pallas_call(
        _ffn_kernel,
        out_shape=jax.ShapeDtypeStruct((t, D_MODEL), jnp.float32),
        grid=(t // FFN_TM,),
        in_specs=[
            pl.BlockSpec((FFN_TM, D_MODEL), lambda i: (i, 0)),
            pl.BlockSpec((1, D_MODEL), lambda i: (0, 0)),
            pl.BlockSpec((D_MODEL, 2 * D_FF), lambda i: (0, 0), pipeline_mode=pl.Buffered(1)),
            pl.BlockSpec((D_FF, D_MODEL), lambda i: (0, 0), pipeline_mode=pl.Buffered(1)),
            pl.BlockSpec((1, D_MODEL), lambda i: (0, 0)),
        ],
        out_specs=pl.BlockSpec((FFN_TM, D_MODEL), lambda i: (i, 0)),
        compiler_params=pltpu.CompilerParams(
            dimension_semantics=("arbitrary",), vmem_limit_bytes=VMEM_LIMIT_BYTES),
        name="ffn",
    )(h1, g_ffn, wgu, wd, g_final)


def _rope_tables(seq):
    inv_freq = ROPE_THETA ** (-jnp.arange(0, ROPE_DIM, 2, dtype=jnp.float32) / ROPE_DIM)
    freq = jnp.concatenate([inv_freq, inv_freq, jnp.zeros((HEAD_DIM - ROPE_DIM,), jnp.float32)])

    def cos_sin(pos, axis):
        ang = pos.astype(jnp.float32)[:, None] * freq[None, :]
        return jnp.stack([jnp.cos(ang), jnp.sin(ang)], axis=axis)

    return (cos_sin(jnp.arange(PROJ_TM, dtype=jnp.int32), 0),
            cos_sin(jnp.arange(0, seq, PROJ_TM, dtype=jnp.int32), 1))


def kernel(x, g_mix, w_in, conv_w, w_attn_branch, w_conv_branch, w_out, g_ffn, w_gate_up, w_down, g_final):
    b, s, d = x.shape
    depth = w_in.shape[0]
    assert d == D_MODEL and s % PROJ_TM == 0 and s % (ATTN_PAIR * MOBA_BLOCK) == 0
    bf16 = jnp.bfloat16
    tables = _rope_tables(s)
    h = x.reshape(b * s, d)
    for l in range(depth):
        proj = _in_proj(h, g_mix[l][None, :], w_in[l], tables, s)
        attn = _moba_attention(proj.reshape(b, s, IN_WIDTH))
        h = _mix(h, attn.reshape(b * s, d), proj, conv_w[l], w_attn_branch[l].astype(bf16),
                 w_conv_branch[l].astype(bf16), w_out[l].astype(bf16), s)
        last = l == depth - 1
        assert last, "only DEPTH == 1 is supported"
        h = _ffn(h, g_ffn[l][None, :], w_gate_up[l].astype(bf16), w_down[l].astype(bf16), g_final[None, :])
    return h.reshape(b, s, d)
```

```python
import functools
import math

import jax
import jax.numpy as jnp
import numpy as np
from jax import lax
from jax.experimental import pallas as pl
from jax.experimental.pallas import tpu as pltpu

D_MODEL = 1024
N_HEADS = 8
HEAD_DIM = 128
MOBA_BLOCK = 256
MOBA_TOPK = 3
ROPE_THETA = 500000.0
ROPE_DIM = HEAD_DIM // 4
ROPE_HALF = ROPE_DIM // 2
CONV_K = 3
D_FF = 2816
EPS = 1e-6
NEG = -1e30
N_SPLITS = 8
IN_WIDTH = N_SPLITS * D_MODEL

VMEM_LIMIT_BYTES = 56 * 1024 * 1024
PROJ_TM = 2048
PROJ_SUB = 256
MIX_TM = 512
FFN_TM = 512
FFN_CHUNKS = 1
ATTN_HEADS = 4
ATTN_PAIR = 2
ATTN_SLOTS = 4
ATTN_TRIP = 8
SUBLANES = 8
VT_ROWS = HEAD_DIM + 2 * SUBLANES
HALO = SUBLANES


def _rms(xf, g):
    return xf * lax.rsqrt(jnp.mean(xf * xf, axis=-1, keepdims=True) + EPS) * g


def _in_proj_kernel(x_ref, g_ref, w_ref, cosb_ref, sinb_ref, rot_ref, o_ref, u_ref):
    j = pl.program_id(1)

    @pl.when(j == 0)
    def _():
        u_ref[...] = _rms(x_ref[...], g_ref[...]).astype(jnp.bfloat16)

    def pieces(store):
        for c in range(D_MODEL // PROJ_SUB):
            w = w_ref[:, c * PROJ_SUB:(c + 1) * PROJ_SUB].astype(jnp.bfloat16)
            store(jnp.dot(u_ref[...], w, preferred_element_type=jnp.float32), c * PROJ_SUB)

    def rope_store(acc, col0):
        scale = jnp.where(j == 0, math.log2(math.e) / math.sqrt(HEAD_DIM), 1.0)
        cb, sb, ca, sa = cosb_ref[...], sinb_ref[...], rot_ref[0:1, :] * scale, rot_ref[1:2, :] * scale
        cos, sin = cb * ca - sb * sa, sb * ca + cb * sa
        lane = lax.broadcasted_iota(jnp.int32, (1, HEAD_DIM), 1)
        sina = jnp.where(lane < ROPE_HALF, -sin, 0.0)
        sinb = jnp.where(lane < ROPE_HALF, 0.0, sin)
        for h in range(PROJ_SUB // HEAD_DIM):
            t = acc[:, h * HEAD_DIM:(h + 1) * HEAD_DIM]
            r = (t * cos + pltpu.roll(t, HEAD_DIM - ROPE_HALF, 1) * sina
                 + pltpu.roll(t, ROPE_HALF, 1) * sinb)
            o_ref[:, col0 + h * HEAD_DIM:col0 + (h + 1) * HEAD_DIM] = r.astype(o_ref.dtype)

    def plain_store(acc, col0):
        o_ref[:, col0:col0 + PROJ_SUB] = acc.astype(o_ref.dtype)

    @pl.when(j < 2)
    def _():
        pieces(rope_store)

    @pl.when(j >= 2)
    def _():
        pieces(plain_store)


def _in_proj(x2, g_mix, w_in, tables, seq):
    t = x2.shape[0]
    tiles_per_seq = seq // PROJ_TM
    base, rot = tables

    def tab_spec(k):
        return pl.BlockSpec((None, PROJ_TM, HEAD_DIM), lambda i, j: (k, 0, 0))

    return pl.pallas_call(
        _in_proj_kernel,
        out_shape=jax.ShapeDtypeStruct((t, IN_WIDTH), jnp.bfloat16),
        grid=(t // PROJ_TM, N_SPLITS),
        in_specs=[
            pl.BlockSpec((PROJ_TM, D_MODEL), lambda i, j: (i, 0)),
            pl.BlockSpec((1, D_MODEL), lambda i, j: (0, 0)),
            pl.BlockSpec((D_MODEL, D_MODEL), lambda i, j: (0, j)),
            tab_spec(0), tab_spec(1),
            pl.BlockSpec((None, 2, HEAD_DIM), lambda i, j: (i % tiles_per_seq, 0, 0)),
        ],
        out_specs=pl.BlockSpec((PROJ_TM, D_MODEL), lambda i, j: (i, j)),
        scratch_shapes=[pltpu.VMEM((PROJ_TM, D_MODEL), jnp.bfloat16)],
        compiler_params=pltpu.CompilerParams(
            dimension_semantics=("arbitrary", "arbitrary"), vmem_limit_bytes=VMEM_LIMIT_BYTES),
        name="in_proj",
    )(x2, g_mix, w_in, base, base, rot)


def _reduce_rows(x, op, reduce_fn):
    rows = x.shape[0]
    while rows > SUBLANES and rows % (2 * SUBLANES) == 0:
        rows //= 2
        x = op(x[:rows], x[rows:])
    return reduce_fn(x, axis=0, keepdims=True)


def _moba_kernel(q_ref, k_ref, v_ref, o_ref, vt_ref, kmean_ref, sel_ref, qt_ref, stat_ref, *slot_refs, n_blocks):
    i = pl.program_id(2)
    f32, bf16 = jnp.float32, jnp.bfloat16
    heads = range(ATTN_HEADS)
    hsl = [slice(h * HEAD_DIM, (h + 1) * HEAD_DIM) for h in heads]
    nh, ns, gk = ATTN_HEADS, ATTN_SLOTS, ATTN_PAIR * MOBA_BLOCK
    n_pairs_total = n_blocks // ATTN_PAIR
    groups = [slot_refs[g * nh:(g + 1) * nh] for g in range(len(slot_refs) // nh)]
    s_ref = groups[0:2]
    smax_ref = groups[2:4]
    p_ref = groups[4:4 + ns]
    acc_ref = groups[4 + ns]
    pown_ref = groups[5 + ns]

    @pl.when(i == 0)
    def _():
        row = lax.broadcasted_iota(jnp.int32, (VT_ROWS - HEAD_DIM, vt_ref.shape[2]), 0)
        for h in heads:
            vt_ref[h, HEAD_DIM:, :] = jnp.where(row == 0, 1.0, 0.0).astype(bf16)
            for c in range(n_blocks):
                rows = slice(c * MOBA_BLOCK, (c + 1) * MOBA_BLOCK)
                kmean_ref[h, c:c + 1, :] = jnp.mean(k_ref[rows, hsl[h]].astype(f32), axis=0, keepdims=True)
                vt_ref[h, :HEAD_DIM, rows] = v_ref[rows, hsl[h]].astype(f32).T.astype(bf16)
            for slot in range(ns):
                p_ref[slot][h][...] = jnp.zeros(p_ref[slot][h].shape, bf16)

    def krows(idx, size):
        return pl.ds(pl.multiple_of(idx * size, size), size)

    def qk(pair, slot):
        for h in heads:
            s = jnp.dot(k_ref[krows(pair, gk), hsl[h]], qt_ref[h], preferred_element_type=f32)
            s_ref[slot][h][...] = s
            for b in range(ATTN_PAIR):
                blk_s = s[b * MOBA_BLOCK:(b + 1) * MOBA_BLOCK]
                smax_ref[slot][h][b:b + 1, :] = _reduce_rows(blk_s, jnp.maximum, jnp.max)

    def pv(pair, slot):
        return [jnp.dot(vt_ref[h, :, krows(pair, gk)], p_ref[slot][h][...], preferred_element_type=f32)
                for h in heads]

    for h in heads:
        qt_ref[h] = q_ref[:, hsl[h]].astype(f32).T.astype(bf16)
    gates = [jnp.dot(kmean_ref[h].astype(bf16), qt_ref[h], preferred_element_type=f32) for h in heads]
    qk(0, 0)

    for h in heads:
        blk = lax.broadcasted_iota(jnp.int32, gates[h].shape, 0)
        past = blk < i
        g = jnp.where(past, gates[h], NEG)
        sel = jnp.zeros(g.shape, f32)
        for _ in range(MOBA_TOPK):
            mx = jnp.max(g, axis=0, keepdims=True)
            first = jnp.min(jnp.where(g == mx, blk, n_blocks), axis=0, keepdims=True)
            pick = blk == first
            sel = jnp.where(pick, 1.0, sel)
            g = jnp.where(pick, -jnp.inf, g)
        sel_ref[h] = jnp.where(past, sel, 0.0)

        acc_ref[h][...] = jnp.zeros(acc_ref[h].shape, f32)
        start = jnp.full((1, MOBA_BLOCK), NEG, f32)
        stat_ref[h, 0], stat_ref[h, 1], stat_ref[h, 2] = start, jnp.ones_like(start), jnp.ones_like(start)

    def pending(j, pvs):
        return [jnp.where(j >= 2, x, 0.0) for x in pvs]

    def stage(j, r):
        qk(jnp.minimum(j + 1, n_pairs_total - 1), (r + 1) % 2)
        pvs = pending(j, pv(jnp.maximum(j - 2, 0), (r - 2) % ns))
        old = [(stat_ref[h, 0], stat_ref[h, 1], stat_ref[h, 2]) for h in heads]
        for h in heads:
            m, a1, a2 = old[h]
            on = [sel_ref[h, pl.ds(j * ATTN_PAIR + b, 1), :] > 0.0 for b in range(ATTN_PAIR)]
            m_new = m
            for b in range(ATTN_PAIR):
                m_new = jnp.maximum(m_new, jnp.where(on[b], smax_ref[r % 2][h][b:b + 1, :], NEG))
            for b in range(ATTN_PAIR):
                rows = slice(b * MOBA_BLOCK, (b + 1) * MOBA_BLOCK)
                shift = jnp.where(on[b], m_new, -NEG)
                p_ref[r][h][rows, :] = jnp.exp2((s_ref[r % 2][h][rows, :] - shift).astype(bf16))
            stat_ref[h, 0], stat_ref[h, 1], stat_ref[h, 2] = m_new, jnp.exp2(m - m_new), a1
            acc_ref[h][...] = a2 * acc_ref[h][...] + pvs[h]

    def trip(t, carry):
        for r in range(ATTN_TRIP):
            stage(ATTN_TRIP * t + r, r % ns)
        return carry

    n_pairs = (i + ATTN_PAIR - 1) // ATTN_PAIR
    n_trips = n_pairs // ATTN_TRIP
    done = n_trips * ATTN_TRIP
    lax.fori_loop(0, n_trips, trip, 0)
    for r in range(ATTN_TRIP):
        @pl.when(n_pairs - done == r)
        def _():
            own_s = [jnp.dot(k_ref[krows(i, MOBA_BLOCK), hsl[h]], qt_ref[h], preferred_element_type=f32)
                     for h in heads]
            for k in range(r):
                stage(done + k, k % ns)
            pv2 = pending(n_pairs, pv(jnp.maximum(n_pairs - 2, 0), (r - 2) % ns))
            pv1 = pending(n_pairs + 1, pv(jnp.maximum(n_pairs - 1, 0), (r - 1) % ns))
            for h in heads:
                key_idx = lax.broadcasted_iota(jnp.int32, own_s[h].shape, 0)
                qry_idx = lax.broadcasted_iota(jnp.int32, own_s[h].shape, 1)
                st = jnp.where(key_idx <= qry_idx, own_s[h], NEG)
                m = stat_ref[h, 0]
                m_fin = jnp.maximum(m, _reduce_rows(st, jnp.maximum, jnp.max))
                pown_ref[h][...] = jnp.exp2((st - m_fin).astype(bf16))
                pv0 = jnp.dot(vt_ref[h, :, krows(i, MOBA_BLOCK)], pown_ref[h][...], preferred_element_type=f32)
                a1, a2 = stat_ref[h, 1], stat_ref[h, 2]
                acc = jnp.exp2(m - m_fin) * (a1 * (a2 * acc_ref[h][...] + pv2[h]) + pv1[h]) + pv0
                out = acc[:HEAD_DIM] / acc[HEAD_DIM:HEAD_DIM + 1]
                o_ref[:, hsl[h]] = out.T.astype(o_ref.dtype)


def _moba_attention(proj3):
    b, s, _ = proj3.shape
    n_blocks = s // MOBA_BLOCK
    width = ATTN_HEADS * HEAD_DIM
    head_groups = N_HEADS // ATTN_HEADS
    return pl.pallas_call(
        functools.partial(_moba_kernel, n_blocks=n_blocks),
        out_shape=jax.ShapeDtypeStruct((b, s, N_HEADS * HEAD_DIM), jnp.bfloat16),
        grid=(b, head_groups, n_blocks),
        in_specs=[
            pl.BlockSpec((None, MOBA_BLOCK, width), lambda bi, h, i: (bi, i, h)),
            pl.BlockSpec((None, s, width), lambda bi, h, i: (bi, 0, head_groups + h)),
            pl.BlockSpec((None, s, width), lambda bi, h, i: (bi, 0, 2 * head_groups + h)),
        ],
        out_specs=pl.BlockSpec((None, MOBA_BLOCK, width), lambda bi, h, i: (bi, i, h)),
        scratch_shapes=[
            pltpu.VMEM((ATTN_HEADS, VT_ROWS, s), jnp.bfloat16),
            pltpu.VMEM((ATTN_HEADS, n_blocks, HEAD_DIM), jnp.float32),
            pltpu.VMEM((ATTN_HEADS, n_blocks, MOBA_BLOCK), jnp.float32),
            pltpu.VMEM((ATTN_HEADS, HEAD_DIM, MOBA_BLOCK), jnp.bfloat16),
            pltpu.VMEM((ATTN_HEADS, 3, 1, MOBA_BLOCK), jnp.float32),
        ] + [pltpu.VMEM((ATTN_PAIR * MOBA_BLOCK, MOBA_BLOCK), jnp.float32)] * (2 * ATTN_HEADS)
          + [pltpu.VMEM((ATTN_PAIR, MOBA_BLOCK), jnp.float32)] * (2 * ATTN_HEADS)
          + [pltpu.VMEM((ATTN_PAIR * MOBA_BLOCK, MOBA_BLOCK), jnp.bfloat16)] * (ATTN_SLOTS * ATTN_HEADS)
          + [pltpu.VMEM((VT_ROWS, MOBA_BLOCK), jnp.float32)] * ATTN_HEADS
          + [pltpu.VMEM((MOBA_BLOCK, MOBA_BLOCK), jnp.bfloat16)] * ATTN_HEADS,
        compiler_params=pltpu.CompilerParams(
            dimension_semantics=("arbitrary", "arbitrary", "arbitrary"), vmem_limit_bytes=VMEM_LIMIT_BYTES),
        name="moba_attn",
    )(proj3, proj3, proj3)


def _mix_kernel(x_ref, attn_ref, cg_ref, bg_ref, xc_ref, ga_ref, gc_ref, cgh_ref, xch_ref,
                cw_ref, wab_ref, wcb_ref, wout_ref, o_ref, *, tiles_per_seq):
    f32 = jnp.float32
    i = pl.program_id(0)
    cx = cg_ref[...].astype(f32) * xc_ref[...].astype(f32)
    live = (i % tiles_per_seq != 0).astype(f32)
    halo = cgh_ref[...].astype(f32) * xch_ref[...].astype(f32) * live
    row = lax.broadcasted_iota(jnp.int32, cx.shape, 0)
    cw = cw_ref[...]
    z = cw[CONV_K - 1:CONV_K, :] * cx
    for d in range(1, CONV_K):
        shifted = pltpu.roll(cx, d, 0)
        for r in range(d):
            shifted = jnp.where(row == r, halo[HALO - d + r:HALO - d + r + 1, :], shifted)
        z = z + cw[CONV_K - 1 - d:CONV_K - d, :] * shifted
    yc = jnp.dot((bg_ref[...].astype(f32) * z).astype(jnp.bfloat16), wcb_ref[...], preferred_element_type=f32)
    ya = jnp.dot(attn_ref[...], wab_ref[...], preferred_element_type=f32)
    merged = jax.nn.sigmoid(ga_ref[...].astype(f32)) * ya + jax.nn.sigmoid(gc_ref[...].astype(f32)) * yc
    o_ref[...] = x_ref[...] + jnp.dot(merged.astype(jnp.bfloat16), wout_ref[...], preferred_element_type=f32)


def _mix(x2, attn2, proj, conv_w, wab, wcb, wout, seq):
    t = x2.shape[0]
    tiles_per_seq = seq // MIX_TM
    halo_per_tile = MIX_TM // HALO

    def col(c):
        return pl.BlockSpec((MIX_TM, D_MODEL), lambda i: (i, c))

    def halo(c):
        return pl.BlockSpec((HALO, D_MODEL), lambda i: (jnp.maximum(i * halo_per_tile - 1, 0), c))

    wspec = pl.BlockSpec((D_MODEL, D_MODEL), lambda i: (0, 0))
    return pl.pallas_call(
        functools.partial(_mix_kernel, tiles_per_seq=tiles_per_seq),
        out_shape=jax.ShapeDtypeStruct((t, D_MODEL), jnp.float32),
        grid=(t // MIX_TM,),
        in_specs=[
            pl.BlockSpec((MIX_TM, D_MODEL), lambda i: (i, 0)),
            pl.BlockSpec((MIX_TM, D_MODEL), lambda i: (i, 0)),
            col(3), col(4), col(5), col(6), col(7),
            halo(3), halo(5),
            pl.BlockSpec((CONV_K, D_MODEL), lambda i: (0, 0)),
            wspec, wspec, wspec,
        ],
        out_specs=pl.BlockSpec((MIX_TM, D_MODEL), lambda i: (i, 0)),
        compiler_params=pltpu.CompilerParams(
            dimension_semantics=("arbitrary",), vmem_limit_bytes=VMEM_LIMIT_BYTES),
        name="mix",
    )(x2, attn2, proj, proj, proj, proj, proj, proj, proj, conv_w, wab, wcb, wout)


def _ffn_kernel(h_ref, g_ref, wgu_ref, wd_ref, gf_ref, o_ref):
    f32 = jnp.float32
    h = h_ref[...]
    u = _rms(h, g_ref[...]).astype(jnp.bfloat16)
    ck = D_FF // FFN_CHUNKS
    acc = h
    for c in range(FFN_CHUNKS):
        gate = jnp.dot(u, wgu_ref[:, c * ck:(c + 1) * ck], preferred_element_type=f32)
        up = jnp.dot(u, wgu_ref[:, D_FF + c * ck:D_FF + (c + 1) * ck], preferred_element_type=f32)
        act = (gate * jax.nn.sigmoid(gate) * up).astype(jnp.bfloat16)
        acc = acc + jnp.dot(act, wd_ref[c * ck:(c + 1) * ck, :], preferred_element_type=f32)
    o_ref[...] = _rms(acc, gf_ref[...])


def _ffn(h1, g_ffn, wgu, wd, g_final):
    t = h1.shape[0]
    return pl.pallas_call(
        _ffn_kernel,
        out_shape=jax.ShapeDtypeStruct((t, D_MODEL), jnp.float32),
        grid=(t // FFN_TM,),
        in_specs=[
            pl.BlockSpec((FFN_TM, D_MODEL), lambda i: (i, 0)),
            pl.BlockSpec((1, D_MODEL), lambda i: (0, 0)),
            pl.BlockSpec((D_MODEL, 2 * D_FF), lambda i: (0, 0), pipeline_mode=pl.Buffered(1)),
            pl.BlockSpec((D_FF, D_MODEL), lambda i: (0, 0), pipeline_mode=pl.Buffered(1)),
            pl.BlockSpec((1, D_MODEL), lambda i: (0, 0)),
        ],
        out_specs=pl.BlockSpec((FFN_TM, D_MODEL), lambda i: (i, 0)),
        compiler_params=pltpu.CompilerParams(
            dimension_semantics=("arbitrary",), vmem_limit_bytes=VMEM_LIMIT_BYTES),
        name="ffn",
    )(h1, g_ffn, wgu, wd, g_final)


def _rope_tables(seq):
    inv_freq = ROPE_THETA ** (-jnp.arange(0, ROPE_DIM, 2, dtype=jnp.float32) / ROPE_DIM)
    freq = jnp.concatenate([inv_freq, inv_freq, jnp.zeros((HEAD_DIM - ROPE_DIM,), jnp.float32)])

    def cos_sin(pos, axis):
        ang = pos.astype(jnp.float32)[:, None] * freq[None, :]
        return jnp.stack([jnp.cos(ang), jnp.sin(ang)], axis=axis)

    return (cos_sin(jnp.arange(PROJ_TM, dtype=jnp.int32), 0),
            cos_sin(jnp.arange(0, seq, PROJ_TM, dtype=jnp.int32), 1))


def kernel(x, g_mix, w_in, conv_w, w_attn_branch, w_conv_branch, w_out, g_ffn, w_gate_up, w_down, g_final):
    b, s, d = x.shape
    depth = w_in.shape[0]
    assert d == D_MODEL and s % PROJ_TM == 0 and s % (ATTN_PAIR * MOBA_BLOCK) == 0
    bf16 = jnp.bfloat16
    tables = _rope_tables(s)
    h = x.reshape(b * s, d)
    for l in range(depth):
        proj = _in_proj(h, g_mix[l][None, :], w_in[l], tables, s)
        attn = _moba_attention(proj.reshape(b, s, IN_WIDTH))
        h = _mix(h, attn.reshape(b * s, d), proj, conv_w[l], w_attn_branch[l].astype(bf16),
                 w_conv_branch[l].astype(bf16), w_out[l].astype(bf16), s)
        last = l == depth - 1
        assert last, "only DEPTH == 1 is supported"
        h = _ffn(h, g_ffn[l][None, :], w_gate_up[l].astype(bf16), w_down[l].astype(bf16), g_final[None, :])
    return h.reshape(b, s, d)
```

```python
import functools
import math

import jax
import jax.numpy as jnp
import numpy as np
from jax import lax
from jax.experimental import pallas as pl
from jax.experimental.pallas import tpu as pltpu

D_MODEL = 1024
N_HEADS = 8
HEAD_DIM = 128
MOBA_BLOCK = 256
MOBA_TOPK = 3
ROPE_THETA = 500000.0
ROPE_DIM = HEAD_DIM // 4
ROPE_HALF = ROPE_DIM // 2
CONV_K = 3
D_FF = 2816
EPS = 1e-6
NEG = -1e30
N_SPLITS = 8
IN_WIDTH = N_SPLITS * D_MODEL

VMEM_LIMIT_BYTES = 56 * 1024 * 1024
PROJ_TM = 2048
PROJ_SUB = 256
MIX_TM = 1024
FFN_TM = 1024
FFN_CHUNKS = 1
ATTN_HEADS = 4
ATTN_PAIR = 2
ATTN_SLOTS = 4
ATTN_TRIP = 8
SUBLANES = 8
VT_ROWS = HEAD_DIM + 2 * SUBLANES
HALO = SUBLANES


def _rms(xf, g):
    return xf * lax.rsqrt(jnp.mean(xf * xf, axis=-1, keepdims=True) + EPS) * g


def _in_proj_kernel(x_ref, g_ref, w_ref, cosb_ref, sinb_ref, rot_ref, o_ref, u_ref):
    j = pl.program_id(1)

    @pl.when(j == 0)
    def _():
        u_ref[...] = _rms(x_ref[...], g_ref[...]).astype(jnp.bfloat16)

    def pieces(store):
        for c in range(D_MODEL // PROJ_SUB):
            w = w_ref[:, c * PROJ_SUB:(c + 1) * PROJ_SUB].astype(jnp.bfloat16)
            store(jnp.dot(u_ref[...], w, preferred_element_type=jnp.float32), c * PROJ_SUB)

    def rope_store(acc, col0):
        scale = jnp.where(j == 0, math.log2(math.e) / math.sqrt(HEAD_DIM), 1.0)
        cb, sb, ca, sa = cosb_ref[...], sinb_ref[...], rot_ref[0:1, :] * scale, rot_ref[1:2, :] * scale
        cos, sin = cb * ca - sb * sa, sb * ca + cb * sa
        lane = lax.broadcasted_iota(jnp.int32, (1, HEAD_DIM), 1)
        sina = jnp.where(lane < ROPE_HALF, -sin, 0.0)
        sinb = jnp.where(lane < ROPE_HALF, 0.0, sin)
        for h in range(PROJ_SUB // HEAD_DIM):
            t = acc[:, h * HEAD_DIM:(h + 1) * HEAD_DIM]
            r = (t * cos + pltpu.roll(t, HEAD_DIM - ROPE_HALF, 1) * sina
                 + pltpu.roll(t, ROPE_HALF, 1) * sinb)
            o_ref[:, col0 + h * HEAD_DIM:col0 + (h + 1) * HEAD_DIM] = r.astype(o_ref.dtype)

    def plain_store(acc, col0):
        o_ref[:, col0:col0 + PROJ_SUB] = acc.astype(o_ref.dtype)

    @pl.when(j < 2)
    def _():
        pieces(rope_store)

    @pl.when(j >= 2)
    def _():
        pieces(plain_store)


def _in_proj(x2, g_mix, w_in, tables, seq):
    t = x2.shape[0]
    tiles_per_seq = seq // PROJ_TM
    base, rot = tables

    def tab_spec(k):
        return pl.BlockSpec((None, PROJ_TM, HEAD_DIM), lambda i, j: (k, 0, 0))

    return pl.pallas_call(
        _in_proj_kernel,
        out_shape=jax.ShapeDtypeStruct((t, IN_WIDTH), jnp.bfloat16),
        grid=(t // PROJ_TM, N_SPLITS),
        in_specs=[
            pl.BlockSpec((PROJ_TM, D_MODEL), lambda i, j: (i, 0)),
            pl.BlockSpec((1, D_MODEL), lambda i, j: (0, 0)),
            pl.BlockSpec((D_MODEL, D_MODEL), lambda i, j: (0, j)),
            tab_spec(0), tab_spec(1),
            pl.BlockSpec((None, 2, HEAD_DIM), lambda i, j: (i % tiles_per_seq, 0, 0)),
        ],
        out_specs=pl.BlockSpec((PROJ_TM, D_MODEL), lambda i, j: (i, j)),
        scratch_shapes=[pltpu.VMEM((PROJ_TM, D_MODEL), jnp.bfloat16)],
        compiler_params=pltpu.CompilerParams(
            dimension_semantics=("arbitrary", "arbitrary"), vmem_limit_bytes=VMEM_LIMIT_BYTES),
        name="in_proj",
    )(x2, g_mix, w_in, base, base, rot)


def _reduce_rows(x, op, reduce_fn):
    rows = x.shape[0]
    while rows > SUBLANES and rows % (2 * SUBLANES) == 0:
        rows //= 2
        x = op(x[:rows], x[rows:])
    return reduce_fn(x, axis=0, keepdims=True)


def _moba_kernel(q_ref, k_ref, v_ref, o_ref, vt_ref, kmean_ref, sel_ref, qt_ref, stat_ref, *slot_refs, n_blocks):
    i = pl.program_id(2)
    f32, bf16 = jnp.float32, jnp.bfloat16
    heads = range(ATTN_HEADS)
    hsl = [slice(h * HEAD_DIM, (h + 1) * HEAD_DIM) for h in heads]
    nh, ns, gk = ATTN_HEADS, ATTN_SLOTS, ATTN_PAIR * MOBA_BLOCK
    n_pairs_total = n_blocks // ATTN_PAIR
    groups = [slot_refs[g * nh:(g + 1) * nh] for g in range(len(slot_refs) // nh)]
    s_ref = groups[0:2]
    smax_ref = groups[2:4]
    p_ref = groups[4:4 + ns]
    acc_ref = groups[4 + ns]
    pown_ref = groups[5 + ns]

    @pl.when(i == 0)
    def _():
        row = lax.broadcasted_iota(jnp.int32, (VT_ROWS - HEAD_DIM, vt_ref.shape[2]), 0)
        for h in heads:
            vt_ref[h, HEAD_DIM:, :] = jnp.where(row == 0, 1.0, 0.0).astype(bf16)
            for c in range(n_blocks):
                rows = slice(c * MOBA_BLOCK, (c + 1) * MOBA_BLOCK)
                kmean_ref[h, c:c + 1, :] = jnp.mean(k_ref[rows, hsl[h]].astype(f32), axis=0, keepdims=True)
                vt_ref[h, :HEAD_DIM, rows] = v_ref[rows, hsl[h]].astype(f32).T.astype(bf16)
            for slot in range(ns):
                p_ref[slot][h][...] = jnp.zeros(p_ref[slot][h].shape, bf16)

    def krows(idx, size):
        return pl.ds(pl.multiple_of(idx * size, size), size)

    def qk(pair, slot):
        for h in heads:
            s = jnp.dot(k_ref[krows(pair, gk), hsl[h]], qt_ref[h], preferred_element_type=f32)
            s_ref[slot][h][...] = s
            for b in range(ATTN_PAIR):
                blk_s = s[b * MOBA_BLOCK:(b + 1) * MOBA_BLOCK]
                smax_ref[slot][h][b:b + 1, :] = _reduce_rows(blk_s, jnp.maximum, jnp.max)

    def pv(pair, slot):
        return [jnp.dot(vt_ref[h, :, krows(pair, gk)], p_ref[slot][h][...], preferred_element_type=f32)
                for h in heads]

    for h in heads:
        qt_ref[h] = q_ref[:, hsl[h]].astype(f32).T.astype(bf16)
    gates = [jnp.dot(kmean_ref[h].astype(bf16), qt_ref[h], preferred_element_type=f32) for h in heads]
    qk(0, 0)

    for h in heads:
        blk = lax.broadcasted_iota(jnp.int32, gates[h].shape, 0)
        past = blk < i
        g = jnp.where(past, gates[h], NEG)
        sel = jnp.zeros(g.shape, f32)
        for _ in range(MOBA_TOPK):
            mx = jnp.max(g, axis=0, keepdims=True)
            first = jnp.min(jnp.where(g == mx, blk, n_blocks), axis=0, keepdims=True)
            pick = blk == first
            sel = jnp.where(pick, 1.0, sel)
            g = jnp.where(pick, -jnp.inf, g)
        sel_ref[h] = jnp.where(past, sel, 0.0)

        acc_ref[h][...] = jnp.zeros(acc_ref[h].shape, f32)
        start = jnp.full((1, MOBA_BLOCK), NEG, f32)
        stat_ref[h, 0], stat_ref[h, 1], stat_ref[h, 2] = start, jnp.ones_like(start), jnp.ones_like(start)

    def pending(j, pvs):
        return [jnp.where(j >= 2, x, 0.0) for x in pvs]

    def stage(j, r):
        qk(jnp.minimum(j + 1, n_pairs_total - 1), (r + 1) % 2)
        pvs = pending(j, pv(jnp.maximum(j - 2, 0), (r - 2) % ns))
        old = [(stat_ref[h, 0], stat_ref[h, 1], stat_ref[h, 2]) for h in heads]
        for h in heads:
            m, a1, a2 = old[h]
            on = [sel_ref[h, pl.ds(j * ATTN_PAIR + b, 1), :] > 0.0 for b in range(ATTN_PAIR)]
            m_new = m
            for b in range(ATTN_PAIR):
                m_new = jnp.maximum(m_new, jnp.where(on[b], smax_ref[r % 2][h][b:b + 1, :], NEG))
            for b in range(ATTN_PAIR):
                rows = slice(b * MOBA_BLOCK, (b + 1) * MOBA_BLOCK)
                shift = jnp.where(on[b], m_new, -NEG)
                p_ref[r][h][rows, :] = jnp.exp2((s_ref[r % 2][h][rows, :] - shift).astype(bf16))
            stat_ref[h, 0], stat_ref[h, 1], stat_ref[h, 2] = m_new, jnp.exp2(m - m_new), a1
            acc_ref[h][...] = a2 * acc_ref[h][...] + pvs[h]

    def trip(t, carry):
        for r in range(ATTN_TRIP):
            stage(ATTN_TRIP * t + r, r % ns)
        return carry

    n_pairs = (i + ATTN_PAIR - 1) // ATTN_PAIR
    n_trips = n_pairs // ATTN_TRIP
    done = n_trips * ATTN_TRIP
    lax.fori_loop(0, n_trips, trip, 0)
    for r in range(ATTN_TRIP):
        @pl.when(n_pairs - done == r)
        def _():
            own_s = [jnp.dot(k_ref[krows(i, MOBA_BLOCK), hsl[h]], qt_ref[h], preferred_element_type=f32)
                     for h in heads]
            for k in range(r):
                stage(done + k, k % ns)
            pv2 = pending(n_pairs, pv(jnp.maximum(n_pairs - 2, 0), (r - 2) % ns))
            pv1 = pending(n_pairs + 1, pv(jnp.maximum(n_pairs - 1, 0), (r - 1) % ns))
            for h in heads:
                key_idx = lax.broadcasted_iota(jnp.int32, own_s[h].shape, 0)
                qry_idx = lax.broadcasted_iota(jnp.int32, own_s[h].shape, 1)
                st = jnp.where(key_idx <= qry_idx, own_s[h], NEG)
                m = stat_ref[h, 0]
                m_fin = jnp.maximum(m, _reduce_rows(st, jnp.maximum, jnp.max))
                pown_ref[h][...] = jnp.exp2((st - m_fin).astype(bf16))
                pv0 = jnp.dot(vt_ref[h, :, krows(i, MOBA_BLOCK)], pown_ref[h][...], preferred_element_type=f32)
                a1, a2 = stat_ref[h, 1], stat_ref[h, 2]
                acc = jnp.exp2(m - m_fin) * (a1 * (a2 * acc_ref[h][...] + pv2[h]) + pv1[h]) + pv0
                out = acc[:HEAD_DIM] / acc[HEAD_DIM:HEAD_DIM + 1]
                o_ref[:, hsl[h]] = out.T.astype(o_ref.dtype)


def _moba_attention(proj3):
    b, s, _ = proj3.shape
    n_blocks = s // MOBA_BLOCK
    width = ATTN_HEADS * HEAD_DIM
    head_groups = N_HEADS // ATTN_HEADS
    return pl.pallas_call(
        functools.partial(_moba_kernel, n_blocks=n_blocks),
        out_shape=jax.ShapeDtypeStruct((b, s, N_HEADS * HEAD_DIM), jnp.bfloat16),
        grid=(b, head_groups, n_blocks),
        in_specs=[
            pl.BlockSpec((None, MOBA_BLOCK, width), lambda bi, h, i: (bi, i, h)),
            pl.BlockSpec((None, s, width), lambda bi, h, i: (bi, 0, head_groups + h)),
            pl.BlockSpec((None, s, width), lambda bi, h, i: (bi, 0, 2 * head_groups + h)),
        ],
        out_specs=pl.BlockSpec((None, MOBA_BLOCK, width), lambda bi, h, i: (bi, i, h)),
        scratch_shapes=[
            pltpu.VMEM((ATTN_HEADS, VT_ROWS, s), jnp.bfloat16),
            pltpu.VMEM((ATTN_HEADS, n_blocks, HEAD_DIM), jnp.float32),
            pltpu.VMEM((ATTN_HEADS, n_blocks, MOBA_BLOCK), jnp.float32),
            pltpu.VMEM((ATTN_HEADS, HEAD_DIM, MOBA_BLOCK), jnp.bfloat16),
            pltpu.VMEM((ATTN_HEADS, 3, 1, MOBA_BLOCK), jnp.float32),
        ] + [pltpu.VMEM((ATTN_PAIR * MOBA_BLOCK, MOBA_BLOCK), jnp.float32)] * (2 * ATTN_HEADS)
          + [pltpu.VMEM((ATTN_PAIR, MOBA_BLOCK), jnp.float32)] * (2 * ATTN_HEADS)
          + [pltpu.VMEM((ATTN_PAIR * MOBA_BLOCK, MOBA_BLOCK), jnp.bfloat16)] * (ATTN_SLOTS * ATTN_HEADS)
          + [pltpu.VMEM((VT_ROWS, MOBA_BLOCK), jnp.float32)] * ATTN_HEADS
          + [pltpu.VMEM((MOBA_BLOCK, MOBA_BLOCK), jnp.bfloat16)] * ATTN_HEADS,
        compiler_params=pltpu.CompilerParams(
            dimension_semantics=("arbitrary", "arbitrary", "arbitrary"), vmem_limit_bytes=VMEM_LIMIT_BYTES),
        name="moba_attn",
    )(proj3, proj3, proj3)


def _mix_kernel(x_ref, attn_ref, cg_ref, bg_ref, xc_ref, ga_ref, gc_ref, cgh_ref, xch_ref,
                cw_ref, wab_ref, wcb_ref, wout_ref, o_ref, *, tiles_per_seq):
    f32 = jnp.float32
    i = pl.program_id(0)
    cx = cg_ref[...].astype(f32) * xc_ref[...].astype(f32)
    live = (i % tiles_per_seq != 0).astype(f32)
    halo = cgh_ref[...].astype(f32) * xch_ref[...].astype(f32) * live
    row = lax.broadcasted_iota(jnp.int32, cx.shape, 0)
    cw = cw_ref[...]
    z = cw[CONV_K - 1:CONV_K, :] * cx
    for d in range(1, CONV_K):
        shifted = pltpu.roll(cx, d, 0)
        for r in range(d):
            shifted = jnp.where(row == r, halo[HALO - d + r:HALO - d + r + 1, :], shifted)
        z = z + cw[CONV_K - 1 - d:CONV_K - d, :] * shifted
    yc = jnp.dot((bg_ref[...].astype(f32) * z).astype(jnp.bfloat16), wcb_ref[...], preferred_element_type=f32)
    ya = jnp.dot(attn_ref[...], wab_ref[...], preferred_element_type=f32)
    merged = jax.nn.sigmoid(ga_ref[...].astype(f32)) * ya + jax.nn.sigmoid(gc_ref[...].astype(f32)) * yc
    o_ref[...] = x_ref[...] + jnp.dot(merged.astype(jnp.bfloat16), wout_ref[...], preferred_element_type=f32)


def _mix(x2, attn2, proj, conv_w, wab, wcb, wout, seq):
    t = x2.shape[0]
    tiles_per_seq = seq // MIX_TM
    halo_per_tile = MIX_TM // HALO

    def col(c):
        return pl.BlockSpec((MIX_TM, D_MODEL), lambda i: (i, c))

    def halo(c):
        return pl.BlockSpec((HALO, D_MODEL), lambda i: (jnp.maximum(i * halo_per_tile - 1, 0), c))

    wspec = pl.BlockSpec((D_MODEL, D_MODEL), lambda i: (0, 0))
    return pl.pallas_call(
        functools.partial(_mix_kernel, tiles_per_seq=tiles_per_seq),
        out_shape=jax.ShapeDtypeStruct((t, D_MODEL), jnp.float32),
        grid=(t // MIX_TM,),
        in_specs=[
            pl.BlockSpec((MIX_TM, D_MODEL), lambda i: (i, 0)),
            pl.BlockSpec((MIX_TM, D_MODEL), lambda i: (i, 0)),
            col(3), col(4), col(5), col(6), col(7),
            halo(3), halo(5),
            pl.BlockSpec((CONV_K, D_MODEL), lambda i: (0, 0)),
            wspec, wspec, wspec,
        ],
        out_specs=pl.BlockSpec((MIX_TM, D_MODEL), lambda i: (i, 0)),
        compiler_params=pltpu.CompilerParams(
            dimension_semantics=("arbitrary",), vmem_limit_bytes=VMEM_LIMIT_BYTES),
        name="mix",
    )(x2, attn2, proj, proj, proj, proj, proj, proj, proj, conv_w, wab, wcb, wout)


def _ffn_kernel(h_ref, g_ref, wgu_ref, wd_ref, gf_ref, o_ref):
    f32 = jnp.float32
    h = h_ref[...]
    u = _rms(h, g_ref[...]).astype(jnp.bfloat16)
    ck = D_FF // FFN_CHUNKS
    acc = h
    for c in range(FFN_CHUNKS):
        gate = jnp.dot(u, wgu_ref[:, c * ck:(c + 1) * ck], preferred_element_type=f32)
        up = jnp.dot(u, wgu_ref[:, D_FF + c * ck:D_FF + (c + 1) * ck], preferred_element_type=f32)
        act = (gate * jax.nn.sigmoid(gate) * up).astype(jnp.bfloat16)
        acc = acc + jnp.dot(act, wd_ref[c * ck:(c + 1) * ck, :], preferred_element_type=f32)
    o_ref[...] = _rms(acc, gf_ref[...])


def _ffn(h1, g_ffn, wgu, wd, g_final):
    t = h1.shape[0]
    return pl.pallas_call(
        _ffn_kernel,
        out_shape=jax.ShapeDtypeStruct((t, D_MODEL), jnp.float32),
        grid=(t // FFN_TM,),
        in_specs=[
            pl.BlockSpec((FFN_TM, D_MODEL), lambda i: (i, 0)),
            pl.BlockSpec((1, D_MODEL), lambda i: (0, 0)),
            pl.BlockSpec((D_MODEL, 2 * D_FF), lambda i: (0, 0), pipeline_mode=pl.Buffered(1)),
            pl.BlockSpec((D_FF, D_MODEL), lambda i: (0, 0), pipeline_mode=pl.Buffered(1)),
            pl.BlockSpec((1, D_MODEL), lambda i: (0, 0)),
        ],
        out_specs=pl.BlockSpec((FFN_TM, D_MODEL), lambda i: (i, 0)),
        compiler_params=pltpu.CompilerParams(
            dimension_semantics=("arbitrary",), vmem_limit_bytes=VMEM_LIMIT_BYTES),
        name="ffn",
    )(h1, g_ffn, wgu, wd, g_final)


def _rope_tables(seq):
    inv_freq = ROPE_THETA ** (-jnp.arange(0, ROPE_DIM, 2, dtype=jnp.float32) / ROPE_DIM)
    freq = jnp.concatenate([inv_freq, inv_freq, jnp.zeros((HEAD_DIM - ROPE_DIM,), jnp.float32)])

    def cos_sin(pos, axis):
        ang = pos.astype(jnp.float32)[:, None] * freq[None, :]
        return jnp.stack([jnp.cos(ang), jnp.sin(ang)], axis=axis)

    return (cos_sin(jnp.arange(PROJ_TM, dtype=jnp.int32), 0),
            cos_sin(jnp.arange(0, seq, PROJ_TM, dtype=jnp.int32), 1))


def kernel(x, g_mix, w_in, conv_w, w_attn_branch, w_conv_branch, w_out, g_ffn, w_gate_up, w_down, g_final):
    b, s, d = x.shape
    depth = w_in.shape[0]
    assert d == D_MODEL and s % PROJ_TM == 0 and s % (ATTN_PAIR * MOBA_BLOCK) == 0
    bf16 = jnp.bfloat16
    tables = _rope_tables(s)
    h = x.reshape(b * s, d)
    for l in range(depth):
        proj = _in_proj(h, g_mix[l][None, :], w_in[l], tables, s)
        attn = _moba_attention(proj.reshape(b, s, IN_WIDTH))
        h = _mix(h, attn.reshape(b * s, d), proj, conv_w[l], w_attn_branch[l].astype(bf16),
                 w_conv_branch[l].astype(bf16), w_out[l].astype(bf16), s)
        last = l == depth - 1
        assert last, "only DEPTH == 1 is supported"
        h = _ffn(h, g_ffn[l][None, :], w_gate_up[l].astype(bf16), w_down[l].astype(bf16), g_final[None, :])
    return h.reshape(b, s, d)
```

```python
import functools
import math

import jax
import jax.numpy as jnp
import numpy as np
from jax import lax
from jax.experimental import pallas as pl
from jax.experimental.pallas import tpu as pltpu

D_MODEL = 1024
N_HEADS = 8
HEAD_DIM = 128
MOBA_BLOCK = 256
MOBA_TOPK = 3
ROPE_THETA = 500000.0
ROPE_DIM = HEAD_DIM // 4
ROPE_HALF = ROPE_DIM // 2
CONV_K = 3
D_FF = 2816
EPS = 1e-6
NEG = -1e30
N_SPLITS = 8
IN_WIDTH = N_SPLITS * D_MODEL

VMEM_LIMIT_BYTES = 60 * 1024 * 1024
PROJ_TM = 2048
PROJ_SUB = 256
MIX_TM = 1024
FFN_TM = 1024
FFN_CHUNKS = 1
ATTN_HEADS = 4
ATTN_PAIR = 2
ATTN_SLOTS = 4
ATTN_TRIP = 8
SUBLANES = 8
VT_ROWS = HEAD_DIM + 2 * SUBLANES
HALO = SUBLANES


def _rms(xf, g):
    return xf * lax.rsqrt(jnp.mean(xf * xf, axis=-1, keepdims=True) + EPS) * g


def _in_proj_kernel(x_ref, g_ref, w_ref, cosb_ref, sinb_ref, rot_ref, o_ref, u_ref):
    j = pl.program_id(1)

    @pl.when(j == 0)
    def _():
        u_ref[...] = _rms(x_ref[...], g_ref[...]).astype(jnp.bfloat16)

    def pieces(store):
        for c in range(D_MODEL // PROJ_SUB):
            w = w_ref[:, c * PROJ_SUB:(c + 1) * PROJ_SUB].astype(jnp.bfloat16)
            store(jnp.dot(u_ref[...], w, preferred_element_type=jnp.float32), c * PROJ_SUB)

    def rope_store(acc, col0):
        scale = jnp.where(j == 0, math.log2(math.e) / math.sqrt(HEAD_DIM), 1.0)
        cb, sb, ca, sa = cosb_ref[...], sinb_ref[...], rot_ref[0:1, :] * scale, rot_ref[1:2, :] * scale
        cos, sin = cb * ca - sb * sa, sb * ca + cb * sa
        lane = lax.broadcasted_iota(jnp.int32, (1, HEAD_DIM), 1)
        sina = jnp.where(lane < ROPE_HALF, -sin, 0.0)
        sinb = jnp.where(lane < ROPE_HALF, 0.0, sin)
        for h in range(PROJ_SUB // HEAD_DIM):
            t = acc[:, h * HEAD_DIM:(h + 1) * HEAD_DIM]
            r = (t * cos + pltpu.roll(t, HEAD_DIM - ROPE_HALF, 1) * sina
                 + pltpu.roll(t, ROPE_HALF, 1) * sinb)
            o_ref[:, col0 + h * HEAD_DIM:col0 + (h + 1) * HEAD_DIM] = r.astype(o_ref.dtype)

    def plain_store(acc, col0):
        o_ref[:, col0:col0 + PROJ_SUB] = acc.astype(o_ref.dtype)

    @pl.when(j < 2)
    def _():
        pieces(rope_store)

    @pl.when(j >= 2)
    def _():
        pieces(plain_store)


def _in_proj(x2, g_mix, w_in, tables, seq):
    t = x2.shape[0]
    tiles_per_seq = seq // PROJ_TM
    base, rot = tables

    def tab_spec(k):
        return pl.BlockSpec((None, PROJ_TM, HEAD_DIM), lambda i, j: (k, 0, 0))

    return pl.pallas_call(
        _in_proj_kernel,
        out_shape=jax.ShapeDtypeStruct((t, IN_WIDTH), jnp.bfloat16),
        grid=(t // PROJ_TM, N_SPLITS),
        in_specs=[
            pl.BlockSpec((PROJ_TM, D_MODEL), lambda i, j: (i, 0)),
            pl.BlockSpec((1, D_MODEL), lambda i, j: (0, 0)),
            pl.BlockSpec((D_MODEL, D_MODEL), lambda i, j: (0, j)),
            tab_spec(0), tab_spec(1),
            pl.BlockSpec((None, 2, HEAD_DIM), lambda i, j: (i % tiles_per_seq, 0, 0)),
        ],
        out_specs=pl.BlockSpec((PROJ_TM, D_MODEL), lambda i, j: (i, j)),
        scratch_shapes=[pltpu.VMEM((PROJ_TM, D_MODEL), jnp.bfloat16)],
        compiler_params=pltpu.CompilerParams(
            dimension_semantics=("arbitrary", "arbitrary"), vmem_limit_bytes=VMEM_LIMIT_BYTES),
        name="in_proj",
    )(x2, g_mix, w_in, base, base, rot)


def _reduce_rows(x, op, reduce_fn):
    rows = x.shape[0]
    while rows > SUBLANES and rows % (2 * SUBLANES) == 0:
        rows //= 2
        x = op(x[:rows], x[rows:])
    return reduce_fn(x, axis=0, keepdims=True)


def _moba_kernel(q_ref, qn_ref, k_ref, v_ref, o_ref, vt_ref, kmean_ref, sel_ref, qt_ref, stat_ref, *slot_refs,
                 n_blocks):
    i = pl.program_id(2)
    cur, nxt = i % 2, (i + 1) % 2
    f32, bf16 = jnp.float32, jnp.bfloat16
    heads = range(ATTN_HEADS)
    hsl = [slice(h * HEAD_DIM, (h + 1) * HEAD_DIM) for h in heads]
    nh, ns, gk = ATTN_HEADS, ATTN_SLOTS, ATTN_PAIR * MOBA_BLOCK
    n_pairs_total = n_blocks // ATTN_PAIR
    groups = [slot_refs[g * nh:(g + 1) * nh] for g in range(len(slot_refs) // nh)]
    s_ref = groups[0:2]
    smax_ref = groups[2:4]
    p_ref = groups[4:4 + ns]
    acc_ref = groups[4 + ns]
    pown_ref = groups[5 + ns]

    @pl.when(i == 0)
    def _():
        row = lax.broadcasted_iota(jnp.int32, (VT_ROWS - HEAD_DIM, vt_ref.shape[2]), 0)
        for h in heads:
            vt_ref[h, HEAD_DIM:, :] = jnp.where(row == 0, 1.0, 0.0).astype(bf16)
            for c in range(n_blocks):
                rows = slice(c * MOBA_BLOCK, (c + 1) * MOBA_BLOCK)
                kmean_ref[h, c:c + 1, :] = jnp.mean(k_ref[rows, hsl[h]].astype(f32), axis=0, keepdims=True)
                vt_ref[h, :HEAD_DIM, rows] = v_ref[rows, hsl[h]].astype(f32).T.astype(bf16)
            for slot in range(ns):
                p_ref[slot][h][...] = jnp.zeros(p_ref[slot][h].shape, bf16)

    def krows(idx, size):
        return pl.ds(pl.multiple_of(idx * size, size), size)

    def qk(pair, slot, qslot):
        for h in heads:
            s = jnp.dot(k_ref[krows(pair, gk), hsl[h]], qt_ref[qslot, h], preferred_element_type=f32)
            s_ref[slot][h][...] = s
            for b in range(ATTN_PAIR):
                blk_s = s[b * MOBA_BLOCK:(b + 1) * MOBA_BLOCK]
                smax_ref[slot][h][b:b + 1, :] = _reduce_rows(blk_s, jnp.maximum, jnp.max)

    def pv(pair, slot):
        return [jnp.dot(vt_ref[h, :, krows(pair, gk)], p_ref[slot][h][...], preferred_element_type=f32)
                for h in heads]

    def prepare(tile, q_src, slot):
        for h in heads:
            qt_ref[slot, h] = q_src[:, hsl[h]].astype(f32).T.astype(bf16)
        gates = [jnp.dot(kmean_ref[h].astype(bf16), qt_ref[slot, h], preferred_element_type=f32) for h in heads]
        qk(0, 0, slot)
        for h in heads:
            blk = lax.broadcasted_iota(jnp.int32, gates[h].shape, 0)
            past = blk < tile
            g = jnp.where(past, gates[h], NEG)
            sel = jnp.zeros(g.shape, f32)
            for _ in range(MOBA_TOPK):
                mx = jnp.max(g, axis=0, keepdims=True)
                first = jnp.min(jnp.where(g == mx, blk, n_blocks), axis=0, keepdims=True)
                pick = blk == first
                sel = jnp.where(pick, 1.0, sel)
                g = jnp.where(pick, -jnp.inf, g)
            sel_ref[slot, h] = jnp.where(past, sel, 0.0)

            acc_ref[h][...] = jnp.zeros(acc_ref[h].shape, f32)
            start = jnp.full((1, MOBA_BLOCK), NEG, f32)
            stat_ref[h, 0], stat_ref[h, 1], stat_ref[h, 2] = start, jnp.ones_like(start), jnp.ones_like(start)

    @pl.when(i == 0)
    def _():
        prepare(0, q_ref, 0)


    def pending(j, pvs):
        return [jnp.where(j >= 2, x, 0.0) for x in pvs]

    def stage(j, r):
        qk(jnp.minimum(j + 1, n_pairs_total - 1), (r + 1) % 2, cur)
        pvs = pending(j, pv(jnp.maximum(j - 2, 0), (r - 2) % ns))
        old = [(stat_ref[h, 0], stat_ref[h, 1], stat_ref[h, 2]) for h in heads]
        for h in heads:
            m, a1, a2 = old[h]
            on = [sel_ref[cur, h, pl.ds(j * ATTN_PAIR + b, 1), :] > 0.0 for b in range(ATTN_PAIR)]
            m_new = m
            for b in range(ATTN_PAIR):
                m_new = jnp.maximum(m_new, jnp.where(on[b], smax_ref[r % 2][h][b:b + 1, :], NEG))
            for b in range(ATTN_PAIR):
                rows = slice(b * MOBA_BLOCK, (b + 1) * MOBA_BLOCK)
                shift = jnp.where(on[b], m_new, -NEG)
                p_ref[r][h][rows, :] = jnp.exp2((s_ref[r % 2][h][rows, :] - shift).astype(bf16))
            stat_ref[h, 0], stat_ref[h, 1], stat_ref[h, 2] = m_new, jnp.exp2(m - m_new), a1
            acc_ref[h][...] = a2 * acc_ref[h][...] + pvs[h]

    def trip(t, carry):
        for r in range(ATTN_TRIP):
            stage(ATTN_TRIP * t + r, r % ns)
        return carry

    n_pairs = (i + ATTN_PAIR - 1) // ATTN_PAIR
    n_trips = n_pairs // ATTN_TRIP
    done = n_trips * ATTN_TRIP
    lax.fori_loop(0, n_trips, trip, 0)
    for r in range(ATTN_TRIP):
        @pl.when(n_pairs - done == r)
        def _():
            own_s = [jnp.dot(k_ref[krows(i, MOBA_BLOCK), hsl[h]], qt_ref[cur, h], preferred_element_type=f32)
                     for h in heads]
            for k in range(r):
                stage(done + k, k % ns)
            pv2 = pending(n_pairs, pv(jnp.maximum(n_pairs - 2, 0), (r - 2) % ns))
            pv1 = pending(n_pairs + 1, pv(jnp.maximum(n_pairs - 1, 0), (r - 1) % ns))
            for h in heads:
                key_idx = lax.broadcasted_iota(jnp.int32, own_s[h].shape, 0)
                qry_idx = lax.broadcasted_iota(jnp.int32, own_s[h].shape, 1)
                st = jnp.where(key_idx <= qry_idx, own_s[h], NEG)
                m = stat_ref[h, 0]
                m_fin = jnp.maximum(m, _reduce_rows(st, jnp.maximum, jnp.max))
                pown_ref[h][...] = jnp.exp2((st - m_fin).astype(bf16))
                pv0 = jnp.dot(vt_ref[h, :, krows(i, MOBA_BLOCK)], pown_ref[h][...], preferred_element_type=f32)
                a1, a2 = stat_ref[h, 1], stat_ref[h, 2]
                acc = jnp.exp2(m - m_fin) * (a1 * (a2 * acc_ref[h][...] + pv2[h]) + pv1[h]) + pv0
                out = acc[:HEAD_DIM] / acc[HEAD_DIM:HEAD_DIM + 1]
                o_ref[:, hsl[h]] = out.T.astype(o_ref.dtype)
            prepare(i + 1, qn_ref, nxt)


def _moba_attention(proj3):
    b, s, _ = proj3.shape
    n_blocks = s // MOBA_BLOCK
    width = ATTN_HEADS * HEAD_DIM
    head_groups = N_HEADS // ATTN_HEADS
    return pl.pallas_call(
        functools.partial(_moba_kernel, n_blocks=n_blocks),
        out_shape=jax.ShapeDtypeStruct((b, s, N_HEADS * HEAD_DIM), jnp.bfloat16),
        grid=(b, head_groups, n_blocks),
        in_specs=[
            pl.BlockSpec((None, MOBA_BLOCK, width), lambda bi, h, i: (bi, i, h)),
            pl.BlockSpec((None, MOBA_BLOCK, width), lambda bi, h, i: (bi, jnp.minimum(i + 1, n_blocks - 1), h)),
            pl.BlockSpec((None, s, width), lambda bi, h, i: (bi, 0, head_groups + h)),
            pl.BlockSpec((None, s, width), lambda bi, h, i: (bi, 0, 2 * head_groups + h)),
        ],
        out_specs=pl.BlockSpec((None, MOBA_BLOCK, width), lambda bi, h, i: (bi, i, h)),
        scratch_shapes=[
            pltpu.VMEM((ATTN_HEADS, VT_ROWS, s), jnp.bfloat16),
            pltpu.VMEM((ATTN_HEADS, n_blocks, HEAD_DIM), jnp.float32),
            pltpu.VMEM((2, ATTN_HEADS, n_blocks, MOBA_BLOCK), jnp.float32),
            pltpu.VMEM((2, ATTN_HEADS, HEAD_DIM, MOBA_BLOCK), jnp.bfloat16),
            pltpu.VMEM((ATTN_HEADS, 3, 1, MOBA_BLOCK), jnp.float32),
        ] + [pltpu.VMEM((ATTN_PAIR * MOBA_BLOCK, MOBA_BLOCK), jnp.float32)] * (2 * ATTN_HEADS)
          + [pltpu.VMEM((ATTN_PAIR, MOBA_BLOCK), jnp.float32)] * (2 * ATTN_HEADS)
          + [pltpu.VMEM((ATTN_PAIR * MOBA_BLOCK, MOBA_BLOCK), jnp.bfloat16)] * (ATTN_SLOTS * ATTN_HEADS)
          + [pltpu.VMEM((VT_ROWS, MOBA_BLOCK), jnp.float32)] * ATTN_HEADS
          + [pltpu.VMEM((MOBA_BLOCK, MOBA_BLOCK), jnp.bfloat16)] * ATTN_HEADS,
        compiler_params=pltpu.CompilerParams(
            dimension_semantics=("arbitrary", "arbitrary", "arbitrary"), vmem_limit_bytes=VMEM_LIMIT_BYTES),
        name="moba_attn",
    )(proj3, proj3, proj3, proj3)


def _mix_kernel(x_ref, attn_ref, cg_ref, bg_ref, xc_ref, ga_ref, gc_ref, cgh_ref, xch_ref,
                cw_ref, wab_ref, wcb_ref, wout_ref, o_ref, *, tiles_per_seq):
    f32 = jnp.float32
    i = pl.program_id(0)
    cx = cg_ref[...].astype(f32) * xc_ref[...].astype(f32)
    live = (i % tiles_per_seq != 0).astype(f32)
    halo = cgh_ref[...].astype(f32) * xch_ref[...].astype(f32) * live
    row = lax.broadcasted_iota(jnp.int32, cx.shape, 0)
    cw = cw_ref[...]
    z = cw[CONV_K - 1:CONV_K, :] * cx
    for d in range(1, CONV_K):
        shifted = pltpu.roll(cx, d, 0)
        for r in range(d):
            shifted = jnp.where(row == r, halo[HALO - d + r:HALO - d + r + 1, :], shifted)
        z = z + cw[CONV_K - 1 - d:CONV_K - d, :] * shifted
    yc = jnp.dot((bg_ref[...].astype(f32) * z).astype(jnp.bfloat16), wcb_ref[...], preferred_element_type=f32)
    ya = jnp.dot(attn_ref[...], wab_ref[...], preferred_element_type=f32)
    merged = jax.nn.sigmoid(ga_ref[...].astype(f32)) * ya + jax.nn.sigmoid(gc_ref[...].astype(f32)) * yc
    o_ref[...] = x_ref[...] + jnp.dot(merged.astype(jnp.bfloat16), wout_ref[...], preferred_element_type=f32)


def _mix(x2, attn2, proj, conv_w, wab, wcb, wout, seq):
    t = x2.shape[0]
    tiles_per_seq = seq // MIX_TM
    halo_per_tile = MIX_TM // HALO

    def col(c):
        return pl.BlockSpec((MIX_TM, D_MODEL), lambda i: (i, c))

    def halo(c):
        return pl.BlockSpec((HALO, D_MODEL), lambda i: (jnp.maximum(i * halo_per_tile - 1, 0), c))

    wspec = pl.BlockSpec((D_MODEL, D_MODEL), lambda i: (0, 0))
    return pl.pallas_call(
        functools.partial(_mix_kernel, tiles_per_seq=tiles_per_seq),
        out_shape=jax.ShapeDtypeStruct((t, D_MODEL), jnp.float32),
        grid=(t // MIX_TM,),
        in_specs=[
            pl.BlockSpec((MIX_TM, D_MODEL), lambda i: (i, 0)),
            pl.BlockSpec((MIX_TM, D_MODEL), lambda i: (i, 0)),
            col(3), col(4), col(5), col(6), col(7),
            halo(3), halo(5),
            pl.BlockSpec((CONV_K, D_MODEL), lambda i: (0, 0)),
            wspec, wspec, wspec,
        ],
        out_specs=pl.BlockSpec((MIX_TM, D_MODEL), lambda i: (i, 0)),
        compiler_params=pltpu.CompilerParams(
            dimension_semantics=("arbitrary",), vmem_limit_bytes=VMEM_LIMIT_BYTES),
        name="mix",
    )(x2, attn2, proj, proj, proj, proj, proj, proj, proj, conv_w, wab, wcb, wout)


def _ffn_kernel(h_ref, g_ref, wgu_ref, wd_ref, gf_ref, o_ref):
    f32 = jnp.float32
    h = h_ref[...]
    u = _rms(h, g_ref[...]).astype(jnp.bfloat16)
    ck = D_FF // FFN_CHUNKS
    acc = h
    for c in range(FFN_CHUNKS):
        gate = jnp.dot(u, wgu_ref[:, c * ck:(c + 1) * ck], preferred_element_type=f32)
        up = jnp.dot(u, wgu_ref[:, D_FF + c * ck:D_FF + (c + 1) * ck], preferred_element_type=f32)
        act = (gate * jax.nn.sigmoid(gate) * up).astype(jnp.bfloat16)
        acc = acc + jnp.dot(act, wd_ref[c * ck:(c + 1) * ck, :], preferred_element_type=f32)
    o_ref[...] = _rms(acc, gf_ref[...])


def _ffn(h1, g_ffn, wgu, wd, g_final):
    t = h1.shape[0]
    return pl.pallas_call(
        _ffn_kernel,
        out_shape=jax.ShapeDtypeStruct((t, D_MODEL), jnp.float32),
        grid=(t // FFN_TM,),
        in_specs=[
            pl.BlockSpec((FFN_TM, D_MODEL), lambda i: (i, 0)),
            pl.BlockSpec((1, D_MODEL), lambda i: (0, 0)),
            pl.BlockSpec((D_MODEL, 2 * D_FF), lambda i: (0, 0), pipeline_mode=pl.Buffered(1)),
            pl.BlockSpec((D_FF, D_MODEL), lambda i: (0, 0), pipeline_mode=pl.Buffered(1)),
            pl.BlockSpec((1, D_MODEL), lambda i: (0, 0)),
        ],
        out_specs=pl.BlockSpec((FFN_TM, D_MODEL), lambda i: (i, 0)),
        compiler_params=pltpu.CompilerParams(
            dimension_semantics=("arbitrary",), vmem_limit_bytes=VMEM_LIMIT_BYTES),
        name="ffn",
    )(h1, g_ffn, wgu, wd, g_final)


def _rope_tables(seq):
    inv_freq = ROPE_THETA ** (-jnp.arange(0, ROPE_DIM, 2, dtype=jnp.float32) / ROPE_DIM)
    freq = jnp.concatenate([inv_freq, inv_freq, jnp.zeros((HEAD_DIM - ROPE_DIM,), jnp.float32)])

    def cos_sin(pos, axis):
        ang = pos.astype(jnp.float32)[:, None] * freq[None, :]
        return jnp.stack([jnp.cos(ang), jnp.sin(ang)], axis=axis)

    return (cos_sin(jnp.arange(PROJ_TM, dtype=jnp.int32), 0),
            cos_sin(jnp.arange(0, seq, PROJ_TM, dtype=jnp.int32), 1))


def kernel(x, g_mix, w_in, conv_w, w_attn_branch, w_conv_branch, w_out, g_ffn, w_gate_up, w_down, g_final):
    b, s, d = x.shape
    depth = w_in.shape[0]
    assert d == D_MODEL and s % PROJ_TM == 0 and s % (ATTN_PAIR * MOBA_BLOCK) == 0
    bf16 = jnp.bfloat16
    tables = _rope_tables(s)
    h = x.reshape(b * s, d)
    for l in range(depth):
        proj = _in_proj(h, g_mix[l][None, :], w_in[l], tables, s)
        attn = _moba_attention(proj.reshape(b, s, IN_WIDTH))
        h = _mix(h, attn.reshape(b * s, d), proj, conv_w[l], w_attn_branch[l].astype(bf16),
                 w_conv_branch[l].astype(bf16), w_out[l].astype(bf16), s)
        last = l == depth - 1
        assert last, "only DEPTH == 1 is supported"
        h = _ffn(h, g_ffn[l][None, :], w_gate_up[l].astype(bf16), w_down[l].astype(bf16), g_final[None, :])
    return h.reshape(b, s, d)
```
